```python
import jax, jax.numpy as jnp
from jax import lax
import numpy as np

D_MODEL = 2048
BATCH = 2
SEQ = 8192
DEPTH = 4

PLE_DIM = 256
EPS = 1e-6
MIX = D_MODEL

CHUNK = 128
A_HEAD = 128
A_WIDTH = MIX // 4
A_HEADS = A_WIDTH // A_HEAD

B_GROUP = 128
B_WIDTH = MIX // 4
B_GROUPS = B_WIDTH // B_GROUP
CONV_W = 3

C_WIDTH = MIX - A_WIDTH - B_WIDTH
C_V = 128
C_HEADS = C_WIDTH // C_V
C_NOPE = 128
C_ROPE = 64
KV_RANK = 512
ROPE_BASE = 10000.0
Q_BLOCK = 128

IN_SPLITS = (A_WIDTH, A_WIDTH, A_WIDTH,
             B_WIDTH, B_WIDTH, B_WIDTH, B_WIDTH,
             C_HEADS * (C_NOPE + C_ROPE), KV_RANK, C_ROPE, C_WIDTH)
IN_WIDTH = 3 * A_WIDTH + 4 * B_WIDTH + C_HEADS * (C_NOPE + C_ROPE) + KV_RANK + C_ROPE + C_WIDTH

kernel_name = 'hybrid_sgu_shortconv_mla_encoder'


def rms_norm(x, g):
    x32 = x.astype(jnp.float32)
    y = x32 * lax.rsqrt(jnp.mean(x32 * x32, axis=-1, keepdims=True) + EPS)
    return y.astype(x.dtype) * g


def rope_tables(positions):
    inv = 1.0 / (ROPE_BASE ** (jnp.arange(0, C_ROPE, 2, dtype=jnp.float32) / C_ROPE))
    ang = positions.astype(jnp.float32)[..., None] * inv
    return jnp.cos(ang), jnp.sin(ang)


def apply_rope(x, cos, sin):
    half = x.shape[-1] // 2
    x1, x2 = x[..., :half], x[..., half:]
    out = jnp.concatenate([x1 * cos - x2 * sin, x2 * cos + x1 * sin], axis=-1)
    return out.astype(x.dtype)


def spatial_gating(u, v, z, v_gain, w_s, b_s):
    bsz, s_len, _ = u.shape
    v = rms_norm(v.reshape(bsz, s_len, A_HEADS, A_HEAD), v_gain)
    vc = v.reshape(bsz, s_len // CHUNK, CHUNK, A_HEADS, A_HEAD)
    s = jnp.einsum('hnm,bkmhc->bknhc', w_s, vc) + b_s.T[None, None, :, :, None]
    return u * s.reshape(bsz, s_len, A_WIDTH) * jax.nn.silu(z)


def short_conv(gate_b, gate_c, h, z, conv_w, conv_b):
    s_len = h.shape[1]
    pad = CONV_W // 2
    xp = jnp.pad(gate_c * h, ((0, 0), (pad, pad), (0, 0)))
    y = conv_b + sum(xp[:, j:j + s_len] * conv_w[j] for j in range(CONV_W))
    return gate_b * y * jax.nn.silu(z)


def latent_attention(q, c_kv, k_rope, z, cos, sin, kv_gain, w_ukv, qn_g, qr_g, kn_g, kr_g):
    bsz, s_len, _ = q.shape
    q = q.reshape(bsz, s_len, C_HEADS, C_NOPE + C_ROPE)
    q_nope = rms_norm(q[..., :C_NOPE], qn_g)
    q_rope = apply_rope(rms_norm(q[..., C_NOPE:], qr_g), cos[:, :, None], sin[:, :, None])
    kv = (rms_norm(c_kv, kv_gain) @ w_ukv).reshape(bsz, s_len, C_HEADS, C_NOPE + C_V)
    k_nope = rms_norm(kv[..., :C_NOPE], kn_g)
    v = kv[..., C_NOPE:]
    k_r = apply_rope(rms_norm(k_rope, kr_g), cos, sin)
    scale = (C_NOPE + C_ROPE) ** -0.5
    n_blk = s_len // Q_BLOCK

    def to_blocks(t):
        return jnp.moveaxis(t.reshape(bsz, n_blk, Q_BLOCK, *t.shape[2:]), 1, 0)

    def attend(blk):
        qn, qr = blk
        s = jnp.einsum('bqhd,bkhd->bhqk', qn, k_nope) + jnp.einsum('bqhr,bkr->bhqk', qr, k_r)
        w = jax.nn.softmax(s.astype(jnp.float32) * scale, axis=-1).astype(v.dtype)
        return jnp.einsum('bhqk,bkhd->bqhd', w, v)

    o = lax.map(attend, (to_blocks(q_nope), to_blocks(q_rope)))
    o = jnp.moveaxis(o, 0, 1).reshape(bsz, s_len, C_WIDTH)
    return o * jax.nn.silu(z)


def setup_inputs(seed: int = 0) -> dict:
    key = jax.random.key(seed)
    ks = jax.random.split(key, 24)
    f32 = jnp.float32

    def nrm(k, shape, scale):
        return jax.random.normal(k, shape, f32) * scale

    def gain(k, shape):
        return 1.0 + 0.01 * jax.random.normal(k, shape, f32)

    return {
        'x': jax.random.normal(ks[0], (BATCH, SEQ, D_MODEL), f32),
        'p': jax.random.normal(ks[1], (DEPTH, BATCH, SEQ, PLE_DIM), f32),
        'positions': jnp.broadcast_to(jnp.arange(SEQ, dtype=jnp.int32), (BATCH, SEQ)),
        'attn_norm': gain(ks[2], (DEPTH, D_MODEL)),
        'w_in': nrm(ks[3], (DEPTH, D_MODEL, IN_WIDTH), D_MODEL ** -0.5),
        'sgu_norm': gain(ks[4], (DEPTH, A_HEADS, A_HEAD)),
        'w_spatial': nrm(ks[5], (DEPTH, A_HEADS, CHUNK, CHUNK), CHUNK ** -0.5),
        'b_spatial': gain(ks[6], (DEPTH, A_HEADS, CHUNK)),
        'conv_w': nrm(ks[7], (DEPTH, CONV_W, B_WIDTH), CONV_W ** -0.5),
        'conv_b': nrm(ks[8], (DEPTH, B_WIDTH), 0.01),
        'kv_norm': gain(ks[9], (DEPTH, KV_RANK)),
        'w_ukv': nrm(ks[10], (DEPTH, KV_RANK, C_HEADS * (C_NOPE + C_V)), KV_RANK ** -0.5),
        'q_nope_norm': gain(ks[11], (DEPTH, C_NOPE)),
        'q_rope_norm': gain(ks[12], (DEPTH, C_ROPE)),
        'k_nope_norm': gain(ks[13], (DEPTH, C_NOPE)),
        'k_rope_norm': gain(ks[14], (DEPTH, C_ROPE)),
        'out_norm': gain(ks[15], (DEPTH, MIX)),
        'w_out': nrm(ks[16], (DEPTH, MIX, D_MODEL), MIX ** -0.5),
        'ple_norm': gain(ks[17], (DEPTH, D_MODEL)),
        'w_ple_gate': nrm(ks[18], (DEPTH, D_MODEL, D_MODEL), D_MODEL ** -0.5),
        'w_ple_proj': nrm(ks[19], (DEPTH, PLE_DIM, D_MODEL), PLE_DIM ** -0.5),
    }


def reference(x, p, positions, attn_norm, w_in, sgu_norm, w_spatial, b_spatial, conv_w, conv_b,
              kv_norm, w_ukv, q_nope_norm, q_rope_norm, k_nope_norm, k_rope_norm,
              out_norm, w_out, ple_norm, w_ple_gate, w_ple_proj):
    cos, sin = rope_tables(positions)
    split_pts = [int(c) for c in np.cumsum(IN_SPLITS)[:-1]]
    out_pts = [A_WIDTH, A_WIDTH + B_WIDTH]
    h = x
    for i in range(DEPTH):
        hn = rms_norm(h, attn_norm[i])
        proj = hn @ w_in[i]
        (a_u, a_v, a_z, b_b, b_c, b_h, b_z, c_q, c_kv, c_kr, c_z) = jnp.split(proj, split_pts, axis=-1)
        y_a = spatial_gating(a_u, a_v, a_z, sgu_norm[i], w_spatial[i], b_spatial[i])
        y_b = short_conv(b_b, b_c, b_h, b_z, conv_w[i], conv_b[i])
        y_c = latent_attention(c_q, c_kv, c_kr, c_z, cos, sin, kv_norm[i], w_ukv[i],
                               q_nope_norm[i], q_rope_norm[i], k_nope_norm[i], k_rope_norm[i])
        g_a, g_b, g_c = jnp.split(out_norm[i], out_pts)
        y = jnp.concatenate([rms_norm(y_a, g_a), rms_norm(y_b, g_b), rms_norm(y_c, g_c)], axis=-1)
        h = h + y @ w_out[i]
        gate = jax.nn.sigmoid(rms_norm(h, ple_norm[i]) @ w_ple_gate[i])
        h = h + gate * (p[i] @ w_ple_proj[i])
    return h
```

```python
import functools
import math

import jax
import jax.numpy as jnp
from jax import lax
from jax.experimental import pallas as pl
from jax.experimental.pallas import tpu as pltpu

F32 = jnp.float32
BF16 = jnp.bfloat16

D_MODEL = 2048
PLE_DIM = 256
EPS = 1e-6
CHUNK = 128
A_HEAD = 128
A_WIDTH = 512
A_HEADS = 4
B_WIDTH = 512
CONV_W = 3
C_WIDTH = 1024
C_V = 128
C_HEADS = 8
C_NOPE = 128
C_ROPE = 64
KV_RANK = 512
ROPE_BASE = 10000.0
QK_PAD = 256

COL_QN = 0
COL_CZ = 1024
COL_QR = 2048
COL_CKV = 2560
COL_AU, COL_AV, COL_AZ = 3072, 3584, 4096
COL_BB, COL_BC, COL_BH, COL_BZ = 4608, 5120, 5632, 6144
COL_KR = 6656
IN_USED = 6784
IN_PAD = 6912

Q_SCALE = (C_NOPE + C_ROPE) ** -0.5 * math.log2(math.e)

VMEM_LIMIT = 56 * 1024 * 1024

TM_IN, TN_IN = 1024, 768
TM_PREP = 512
TQ, TK = 512, 512
TM_MIX = 256
HALO = 16


def _rms(x, gain):
    ms = jnp.mean(x * x, axis=-1, keepdims=True)
    return x * lax.rsqrt(ms + EPS) * gain


def _silu(z):
    return z * jax.nn.sigmoid(z)


def _in_proj_kernel(x_ref, g_ref, w_ref, o_ref, hn_ref):
    @pl.when(pl.program_id(1) == 0)
    def _():
        hn_ref[...] = _rms(x_ref[...], g_ref[...]).astype(BF16)

    o_ref[...] = jnp.dot(hn_ref[...], w_ref[...], preferred_element_type=F32).astype(o_ref.dtype)


def _in_proj(h, gain, w):
    m = h.shape[0]
    return pl.pallas_call(
        _in_proj_kernel,
        grid=(m // TM_IN, IN_PAD // TN_IN),
        in_specs=[
            pl.BlockSpec((TM_IN, D_MODEL), lambda i, j: (i, 0)),
            pl.BlockSpec((1, D_MODEL), lambda i, j: (0, 0)),
            pl.BlockSpec((D_MODEL, TN_IN), lambda i, j: (0, j)),
        ],
        out_specs=pl.BlockSpec((TM_IN, TN_IN), lambda i, j: (i, j)),
        out_shape=jax.ShapeDtypeStruct((m, IN_PAD), BF16),
        scratch_shapes=[pltpu.VMEM((TM_IN, D_MODEL), BF16)],
        compiler_params=pltpu.CompilerParams(
            dimension_semantics=("arbitrary", "arbitrary"), vmem_limit_bytes=VMEM_LIMIT),
        name="in_proj",
    )(h, gain, w)


def _prep_kernel(qn_ref, qr_ref, ckv_ref, kr_ref, cos_ref, sin_ref, gqn_ref, gqr_ref, gkv_ref,
                 wukv_ref, gkn_ref, gkr_ref, q_out, k_out, v_out):
    tm = qn_ref.shape[0]
    cos = cos_ref[...]
    sin = sin_ref[...]
    lane = lax.broadcasted_iota(jnp.int32, (tm, 128), 1)
    low_half = (lane & 32) == 0
    first = lane < 64

    def rope(x):
        swapped = jnp.where(low_half, pltpu.roll(x, 96, 1), pltpu.roll(x, 32, 1))
        return x * cos + swapped * sin

    for hd in range(C_HEADS):
        x = qn_ref[:, hd * C_NOPE:(hd + 1) * C_NOPE].astype(F32)
        q_out[hd, :, 0:C_NOPE] = (_rms(x, gqn_ref[...]) * Q_SCALE).astype(BF16)

    for j in range(C_HEADS // 2):
        x = qr_ref[:, j * 128:(j + 1) * 128].astype(F32)
        x2 = x * x
        s_lo = jnp.sum(jnp.where(first, x2, 0.0), axis=-1, keepdims=True)
        s_hi = jnp.sum(jnp.where(first, 0.0, x2), axis=-1, keepdims=True)
        ms = jnp.where(first, s_lo, s_hi) * (1.0 / C_ROPE)
        r = rope(x * lax.rsqrt(ms + EPS) * gqr_ref[...]) * Q_SCALE
        q_out[2 * j, :, C_NOPE:QK_PAD] = jnp.where(first, r, 0.0).astype(BF16)
        q_out[2 * j + 1, :, C_NOPE:QK_PAD] = jnp.where(first, 0.0, r).astype(BF16)

    cn = _rms(ckv_ref[...].astype(F32), gkv_ref[...]).astype(BF16)
    kv = jnp.dot(cn, wukv_ref[...], preferred_element_type=F32)
    kr = rope(_rms(kr_ref[...].astype(F32), gkr_ref[...])).astype(BF16)
    for hd in range(C_HEADS):
        kn = kv[:, hd * C_NOPE:(hd + 1) * C_NOPE]
        k_out[hd, :, 0:C_NOPE] = _rms(kn, gkn_ref[...]).astype(BF16)
        k_out[hd, :, C_NOPE:QK_PAD] = kr
        v_out[hd, :, :] = kv[:, C_WIDTH + hd * C_V:C_WIDTH + (hd + 1) * C_V].astype(BF16)


def _attn_prep(proj, cos128, sin128, gqn, gqr, gkv, wukv, gkn, gkr, bsz, s_len):
    tm = TM_PREP
    nst = s_len // tm
    row = lambda width, col: pl.BlockSpec((tm, width), lambda i: (i, col // width))
    const = lambda shape: pl.BlockSpec(shape, lambda i: (0,) * len(shape))
    head_out = lambda width: pl.BlockSpec((None, C_HEADS, tm, width), lambda i: (i // nst, 0, i % nst, 0))
    return pl.pallas_call(
        _prep_kernel,
        grid=(bsz * nst,),
        in_specs=[
            row(C_WIDTH, COL_QN), row(C_HEADS * C_ROPE, COL_QR), row(KV_RANK, COL_CKV), row(128, COL_KR),
            pl.BlockSpec((tm, 128), lambda i: (i, 0)), pl.BlockSpec((tm, 128), lambda i: (i, 0)),
            const((1, C_NOPE)), const((1, 128)), const((1, KV_RANK)),
            const((KV_RANK, 2 * C_WIDTH)), const((1, C_NOPE)), const((1, 128)),
        ],
        out_specs=[head_out(QK_PAD), head_out(QK_PAD), head_out(C_V)],
        out_shape=[
            jax.ShapeDtypeStruct((bsz, C_HEADS, s_len, QK_PAD), BF16),
            jax.ShapeDtypeStruct((bsz, C_HEADS, s_len, QK_PAD), BF16),
            jax.ShapeDtypeStruct((bsz, C_HEADS, s_len, C_V), BF16),
        ],
        compiler_params=pltpu.CompilerParams(
            dimension_semantics=("arbitrary",), vmem_limit_bytes=VMEM_LIMIT),
        name="attn_prep",
    )(proj, proj, proj, proj, cos128, sin128, gqn, gqr, gkv, wukv, gkn, gkr)


def _flash_kernel(q_ref, k_ref, v_ref, o_ref):
    tq = q_ref.shape[0]
    s_len = k_ref.shape[0]
    q = q_ref[...]

    def body(j, carry):
        m, l, acc = carry
        start = pl.multiple_of(j * TK, TK)
        k = k_ref[pl.ds(start, TK), :]
        v = v_ref[pl.ds(start, TK), :]
        s = lax.dot_general(q, k, (((1,), (1,)), ((), ())), preferred_element_type=F32)
        m_new = jnp.maximum(m, jnp.max(s, axis=-1, keepdims=True))
        p = jnp.exp2(s - m_new)
        alpha = jnp.exp2(m - m_new)
        l = alpha * l + jnp.sum(p, axis=-1, keepdims=True)
        acc = alpha * acc + jnp.dot(p.astype(BF16), v, preferred_element_type=F32)
        return m_new, l, acc

    init = (jnp.full((tq, 1), -jnp.inf, F32), jnp.zeros((tq, 1), F32), jnp.zeros((tq, C_V), F32))
    _, l, acc = lax.fori_loop(0, s_len // TK, body, init)
    o_ref[...] = (acc / l).astype(o_ref.dtype)


def _flash(q, k, v):
    bsz, heads, s_len, _ = q.shape
    nq = s_len // TQ
    return pl.pallas_call(
        _flash_kernel,
        grid=(bsz, heads, nq),
        in_specs=[
            pl.BlockSpec((None, None, TQ, QK_PAD), lambda b, hd, i: (b, hd, i, 0)),
            pl.BlockSpec((None, None, s_len, QK_PAD), lambda b, hd, i: (b, hd, 0, 0)),
            pl.BlockSpec((None, None, s_len, C_V), lambda b, hd, i: (b, hd, 0, 0)),
        ],
        out_specs=pl.BlockSpec((TQ, C_V), lambda b, hd, i: (b * nq + i, hd)),
        out_shape=jax.ShapeDtypeStruct((bsz * s_len, C_WIDTH), BF16),
        compiler_params=pltpu.CompilerParams(
            dimension_semantics=("arbitrary", "arbitrary", "arbitrary"), vmem_limit_bytes=VMEM_LIMIT),
        name="flash",
    )(q, k, v)


def _mix_kernel(au_ref, av_ref, az_ref, bb_ref, bc_ref, bh_ref, bz_ref,
                bcp_ref, bhp_ref, bcn_ref, bhn_ref, cz_ref, o_ref, h_ref, p_ref,
                gsgu_ref, ws_ref, bs_ref, cw_ref, cb_ref, gout_ref, wout_ref, gple_ref, wgate_ref, wpp_ref,
                out_ref, ya_ref, y_ref, *, s_len):
    tm = h_ref.shape[0]
    i = pl.program_id(0)

    for c in range(tm // CHUNK):
        rows = slice(c * CHUNK, (c + 1) * CHUNK)
        for hd in range(A_HEADS):
            cols = slice(hd * A_HEAD, (hd + 1) * A_HEAD)
            vn = _rms(av_ref[rows, cols].astype(F32), gsgu_ref[hd:hd + 1, :]).astype(BF16)
            s = jnp.dot(ws_ref[hd], vn, preferred_element_type=F32) + bs_ref[hd]
            ya_ref[rows, cols] = au_ref[rows, cols].astype(F32) * s * _silu(az_ref[rows, cols].astype(F32))
    y_ref[:, 0:A_WIDTH] = _rms(ya_ref[...], gout_ref[:, 0:A_WIDTH]).astype(BF16)

    xc = bc_ref[...].astype(F32) * bh_ref[...].astype(F32)
    row = lax.broadcasted_iota(jnp.int32, xc.shape, 0)
    has_prev = ((i * tm) % s_len != 0).astype(F32)
    has_next = (((i + 1) * tm) % s_len != 0).astype(F32)
    prev_row = bcp_ref[HALO - 1:HALO, :].astype(F32) * bhp_ref[HALO - 1:HALO, :].astype(F32) * has_prev
    next_row = bcn_ref[0:1, :].astype(F32) * bhn_ref[0:1, :].astype(F32) * has_next
    x_prev = jnp.where(row == 0, prev_row, pltpu.roll(xc, 1, 0))
    x_next = jnp.where(row == tm - 1, next_row, pltpu.roll(xc, tm - 1, 0))
    conv = cb_ref[...] + x_prev * cw_ref[0:1, :] + xc * cw_ref[1:2, :] + x_next * cw_ref[2:3, :]
    yb = bb_ref[...].astype(F32) * conv * _silu(bz_ref[...].astype(F32))
    y_ref[:, A_WIDTH:A_WIDTH + B_WIDTH] = _rms(yb, gout_ref[:, A_WIDTH:A_WIDTH + B_WIDTH]).astype(BF16)

    yc = o_ref[...].astype(F32) * _silu(cz_ref[...].astype(F32))
    y_ref[:, A_WIDTH + B_WIDTH:] = _rms(yc, gout_ref[:, A_WIDTH + B_WIDTH:]).astype(BF16)

    hm = h_ref[...] + jnp.dot(y_ref[...], wout_ref[...], preferred_element_type=F32)
    hn = _rms(hm, gple_ref[...]).astype(BF16)
    gate = jax.nn.sigmoid(jnp.dot(hn, wgate_ref[...], preferred_element_type=F32))
    pp = jnp.dot(p_ref[...].astype(BF16), wpp_ref[...], preferred_element_type=F32)
    out_ref[...] = hm + gate * pp


def _mix_out(proj, o, h, p, gsgu, ws, bs, cw, cb, gout, wout, gple, wgate, wpp, s_len):
    m = h.shape[0]
    tm = TM_MIX
    nhalo = tm // HALO
    last_halo = m // HALO - 1
    row = lambda width, col: pl.BlockSpec((tm, width), lambda i: (i, col // width))
    prev = lambda col: pl.BlockSpec((HALO, B_WIDTH), lambda i: (jnp.maximum(i * nhalo - 1, 0), col // B_WIDTH))
    nxt = lambda col: pl.BlockSpec((HALO, B_WIDTH), lambda i: (jnp.minimum((i + 1) * nhalo, last_halo), col // B_WIDTH))
    const = lambda shape: pl.BlockSpec(shape, lambda i: (0,) * len(shape))
    return pl.pallas_call(
        functools.partial(_mix_kernel, s_len=s_len),
        grid=(m // tm,),
        in_specs=[
            row(A_WIDTH, COL_AU), row(A_WIDTH, COL_AV), row(A_WIDTH, COL_AZ),
            row(B_WIDTH, COL_BB), row(B_WIDTH, COL_BC), row(B_WIDTH, COL_BH), row(B_WIDTH, COL_BZ),
            prev(COL_BC), prev(COL_BH), nxt(COL_BC), nxt(COL_BH),
            row(C_WIDTH, COL_CZ),
            pl.BlockSpec((tm, C_WIDTH), lambda i: (i, 0)),
            pl.BlockSpec((tm, D_MODEL), lambda i: (i, 0)),
            pl.BlockSpec((tm, PLE_DIM), lambda i: (i, 0)),
            const((A_HEADS, A_HEAD)), const((A_HEADS, CHUNK, CHUNK)), const((A_HEADS, CHUNK, A_HEAD)),
            const((CONV_W, B_WIDTH)), const((1, B_WIDTH)), const((1, D_MODEL)),
            const((D_MODEL, D_MODEL)), const((1, D_MODEL)), const((D_MODEL, D_MODEL)), const((PLE_DIM, D_MODEL)),
        ],
        out_specs=pl.BlockSpec((tm, D_MODEL), lambda i: (i, 0)),
        out_shape=jax.ShapeDtypeStruct((m, D_MODEL), F32),
        scratch_shapes=[pltpu.VMEM((tm, A_WIDTH), F32), pltpu.VMEM((tm, D_MODEL), BF16)],
        compiler_params=pltpu.CompilerParams(
            dimension_semantics=("arbitrary",), vmem_limit_bytes=VMEM_LIMIT),
        name="mix_out",
    )(proj, proj, proj, proj, proj, proj, proj, proj, proj, proj, proj, proj, o, h, p,
      gsgu, ws, bs, cw, cb, gout, wout, gple, wgate, wpp)


def _permute_w_in(w):
    a0, b0 = 0, 3 * A_WIDTH
    q0 = b0 + 4 * B_WIDTH
    ckv0 = q0 + C_HEADS * (C_NOPE + C_ROPE)
    kr0 = ckv0 + KV_RANK
    cz0 = kr0 + C_ROPE
    q = w[:, q0:ckv0].reshape(D_MODEL, C_HEADS, C_NOPE + C_ROPE)
    qn = q[:, :, :C_NOPE].reshape(D_MODEL, C_HEADS * C_NOPE)
    qr = q[:, :, C_NOPE:].reshape(D_MODEL, C_HEADS * C_ROPE)
    kr = w[:, kr0:cz0]
    pieces = [qn, w[:, cz0:cz0 + C_WIDTH], qr, w[:, ckv0:kr0], w[:, a0:b0], w[:, b0:q0], kr, kr,
              jnp.zeros((D_MODEL, IN_PAD - IN_USED), w.dtype)]
    return jnp.concatenate(pieces, axis=1).astype(BF16)


def _permute_w_ukv(w):
    kv = w.reshape(KV_RANK, C_HEADS, C_NOPE + C_V)
    kn = kv[:, :, :C_NOPE].reshape(KV_RANK, C_HEADS * C_NOPE)
    v = kv[:, :, C_NOPE:].reshape(KV_RANK, C_HEADS * C_V)
    return jnp.concatenate([kn, v], axis=1).astype(BF16)


def kernel(x, p, positions, attn_norm, w_in, sgu_norm, w_spatial, b_spatial, conv_w, conv_b, kv_norm, w_ukv,
           q_nope_norm, q_rope_norm, k_nope_norm, k_rope_norm, out_norm, w_out, ple_norm, w_ple_gate, w_ple_proj):
    bsz, s_len, d = x.shape
    depth = w_in.shape[0]
    m = bsz * s_len

    inv = 1.0 / (ROPE_BASE ** (jnp.arange(0, C_ROPE, 2, dtype=F32) / C_ROPE))
    ang = positions.astype(F32)[..., None] * inv
    cos, sin = jnp.cos(ang), jnp.sin(ang)
    cos128 = jnp.tile(cos, (1, 1, 4)).reshape(m, 128)
    sin128 = jnp.tile(jnp.concatenate([-sin, sin], axis=-1), (1, 1, 2)).reshape(m, 128)

    h = x.reshape(m, d)
    for i in range(depth):
        proj = _in_proj(h, attn_norm[i][None, :], _permute_w_in(w_in[i]))
        q, k, v = _attn_prep(
            proj, cos128, sin128, q_nope_norm[i][None, :], jnp.tile(q_rope_norm[i], 2)[None, :],
            kv_norm[i][None, :], _permute_w_ukv(w_ukv[i]), k_nope_norm[i][None, :],
            jnp.tile(k_rope_norm[i], 2)[None, :], bsz, s_len)
        o = _flash(q, k, v)
        bs_full = jnp.broadcast_to(b_spatial[i][:, :, None], (A_HEADS, CHUNK, A_HEAD))
        h = _mix_out(
            proj, o, h, p[i].reshape(m, PLE_DIM), sgu_norm[i], w_spatial[i].astype(BF16), bs_full,
            conv_w[i], conv_b[i][None, :], out_norm[i][None, :], w_out[i].astype(BF16), ple_norm[i][None, :],
            w_ple_gate[i].astype(BF16), w_ple_proj[i].astype(BF16), s_len)
    return h.reshape(bsz, s_len, d)
```

```python
import functools
import math

import jax
import jax.numpy as jnp
from jax import lax
from jax.experimental import pallas as pl
from jax.experimental.pallas import tpu as pltpu

F32 = jnp.float32
BF16 = jnp.bfloat16

D_MODEL = 2048
PLE_DIM = 256
EPS = 1e-6
CHUNK = 128
A_HEAD = 128
A_WIDTH = 512
A_HEADS = 4
B_WIDTH = 512
CONV_W = 3
C_WIDTH = 1024
C_V = 128
C_HEADS = 8
C_NOPE = 128
C_ROPE = 64
KV_RANK = 512
ROPE_BASE = 10000.0
QK_PAD = 256
V_PAD = 256
MAX_SHIFT = 50.0

COL_QN = 0
COL_CZ = 1024
COL_QR = 2048
COL_CKV = 2560
COL_AU, COL_AV, COL_AZ = 3072, 3584, 4096
COL_BB, COL_BC, COL_BH, COL_BZ = 4608, 5120, 5632, 6144
COL_KR = 6656
IN_USED = 6784
IN_PAD = 6912

Q_SCALE = (C_NOPE + C_ROPE) ** -0.5 * math.log2(math.e)

VMEM_LIMIT = 56 * 1024 * 1024

TM_IN, TN_IN = 1024, 768
TM_PREP = 512
TQ, TK = 512, 512
FLASH_UNROLL = 15
TM_MIX = 256
HALO = 16


def _rms(x, gain):
    ms = jnp.mean(x * x, axis=-1, keepdims=True)
    return x * lax.rsqrt(ms + EPS) * gain


def _silu(z):
    return z * jax.nn.sigmoid(z)


def _in_proj_kernel(x_ref, g_ref, w_ref, o_ref, hn_ref):
    @pl.when(pl.program_id(1) == 0)
    def _():
        hn_ref[...] = _rms(x_ref[...], g_ref[...]).astype(BF16)

    o_ref[...] = jnp.dot(hn_ref[...], w_ref[...], preferred_element_type=F32).astype(o_ref.dtype)


def _in_proj(h, gain, w):
    m = h.shape[0]
    return pl.pallas_call(
        _in_proj_kernel,
        grid=(m // TM_IN, IN_PAD // TN_IN),
        in_specs=[
            pl.BlockSpec((TM_IN, D_MODEL), lambda i, j: (i, 0)),
            pl.BlockSpec((1, D_MODEL), lambda i, j: (0, 0)),
            pl.BlockSpec((D_MODEL, TN_IN), lambda i, j: (0, j)),
        ],
        out_specs=pl.BlockSpec((TM_IN, TN_IN), lambda i, j: (i, j)),
        out_shape=jax.ShapeDtypeStruct((m, IN_PAD), BF16),
        scratch_shapes=[pltpu.VMEM((TM_IN, D_MODEL), BF16)],
        compiler_params=pltpu.CompilerParams(
            dimension_semantics=("arbitrary", "arbitrary"), vmem_limit_bytes=VMEM_LIMIT),
        name="in_proj",
    )(h, gain, w)


def _prep_kernel(qn_ref, qr_ref, ckv_ref, kr_ref, cos_ref, sin_ref, gqn_ref, gqr_ref, gkv_ref,
                 wukv_ref, gkn_ref, gkr_ref, kpad_ref, q_out, k_out, v_out):
    tm = qn_ref.shape[0]
    cos = cos_ref[...]
    sin = sin_ref[...]
    lane = lax.broadcasted_iota(jnp.int32, (tm, 128), 1)
    low_half = (lane & 32) == 0
    first = lane < 64

    def rope(x):
        swapped = jnp.where(low_half, pltpu.roll(x, 96, 1), pltpu.roll(x, 32, 1))
        return x * cos + swapped * sin

    for hd in range(C_HEADS):
        x = qn_ref[:, hd * C_NOPE:(hd + 1) * C_NOPE].astype(F32)
        q_out[hd, :, 0:C_NOPE] = (_rms(x, gqn_ref[...]) * Q_SCALE).astype(BF16)

    q_pad = jnp.where(lane == C_ROPE, 1.0, 0.0)
    for j in range(C_HEADS // 2):
        x = qr_ref[:, j * 128:(j + 1) * 128].astype(F32)
        x2 = x * x
        s_lo = jnp.sum(jnp.where(first, x2, 0.0), axis=-1, keepdims=True)
        s_hi = jnp.sum(jnp.where(first, 0.0, x2), axis=-1, keepdims=True)
        ms = jnp.where(first, s_lo, s_hi) * (1.0 / C_ROPE)
        r = rope(x * lax.rsqrt(ms + EPS) * gqr_ref[...]) * Q_SCALE
        q_out[2 * j, :, C_NOPE:QK_PAD] = jnp.where(first, r, q_pad).astype(BF16)
        q_out[2 * j + 1, :, C_NOPE:QK_PAD] = jnp.where(first, pltpu.roll(r, 64, 1), q_pad).astype(BF16)

    cn = _rms(ckv_ref[...].astype(F32), gkv_ref[...]).astype(BF16)
    kv = jnp.dot(cn, wukv_ref[...], preferred_element_type=F32)
    kr = rope(_rms(kr_ref[...].astype(F32), gkr_ref[...]))
    kr = jnp.where(first, kr, kpad_ref[...]).astype(BF16)
    ones_col = jnp.where(lane == 0, 1.0, 0.0).astype(BF16)
    for hd in range(C_HEADS):
        kn = kv[:, hd * C_NOPE:(hd + 1) * C_NOPE]
        k_out[hd, :, 0:C_NOPE] = _rms(kn, gkn_ref[...]).astype(BF16)
        k_out[hd, :, C_NOPE:QK_PAD] = kr
        v_out[hd, :, 0:C_V] = kv[:, C_WIDTH + hd * C_V:C_WIDTH + (hd + 1) * C_V].astype(BF16)
        v_out[hd, :, C_V:V_PAD] = ones_col


def _attn_prep(proj, cos128, sin128, gqn, gqr, gkv, wukv, gkn, gkr, kpad, bsz, s_len):
    tm = TM_PREP
    nst = s_len // tm
    row = lambda width, col: pl.BlockSpec((tm, width), lambda i: (i, col // width))
    const = lambda shape: pl.BlockSpec(shape, lambda i: (0,) * len(shape))
    head_out = lambda width: pl.BlockSpec((None, C_HEADS, tm, width), lambda i: (i // nst, 0, i % nst, 0))
    return pl.pallas_call(
        _prep_kernel,
        grid=(bsz * nst,),
        in_specs=[
            row(C_WIDTH, COL_QN), row(C_HEADS * C_ROPE, COL_QR), row(KV_RANK, COL_CKV), row(128, COL_KR),
            pl.BlockSpec((tm, 128), lambda i: (i, 0)), pl.BlockSpec((tm, 128), lambda i: (i, 0)),
            const((1, C_NOPE)), const((1, 128)), const((1, KV_RANK)),
            const((KV_RANK, 2 * C_WIDTH)), const((1, C_NOPE)), const((1, 128)), const((1, 128)),
        ],
        out_specs=[head_out(QK_PAD), head_out(QK_PAD), head_out(V_PAD)],
        out_shape=[
            jax.ShapeDtypeStruct((bsz, C_HEADS, s_len, QK_PAD), BF16),
            jax.ShapeDtypeStruct((bsz, C_HEADS, s_len, QK_PAD), BF16),
            jax.ShapeDtypeStruct((bsz, C_HEADS, s_len, V_PAD), BF16),
        ],
        compiler_params=pltpu.CompilerParams(
            dimension_semantics=("arbitrary",), vmem_limit_bytes=VMEM_LIMIT),
        name="attn_prep",
    )(proj, proj, proj, proj, cos128, sin128, gqn, gqr, gkv, wukv, gkn, gkr, kpad)


_NT = (((1,), (1,)), ((), ()))


def _flash_bounded_kernel(q_ref, k_ref, v_ref, o_ref):
    tq = q_ref.shape[0]
    s_len = k_ref.shape[0]
    q = q_ref[...]

    nk = s_len // TK

    def probs(j):
        start = pl.multiple_of(j * TK, TK)
        s = lax.dot_general(q, k_ref[pl.ds(start, TK), :], _NT, preferred_element_type=F32)
        return jnp.exp2(s).astype(BF16)

    def weighted(j, p, acc):
        start = pl.multiple_of(j * TK, TK)
        return acc + jnp.dot(p, v_ref[pl.ds(start, TK), :], preferred_element_type=F32)

    def body(j, carry):
        p, acc = carry
        p_next = probs(j + 1)
        return p_next, weighted(j, p, acc)

    p, acc = lax.fori_loop(0, nk - 1, body, (probs(0), jnp.zeros((tq, V_PAD), F32)), unroll=FLASH_UNROLL)
    acc = weighted(nk - 1, p, acc)
    o_ref[...] = (acc[:, 0:C_V] / acc[:, C_V:C_V + 1]).astype(o_ref.dtype)


def _flash_online_kernel(q_ref, k_ref, v_ref, o_ref):
    tq = q_ref.shape[0]
    s_len = k_ref.shape[0]
    q = q_ref[...]

    def body(j, carry):
        m, acc = carry
        start = pl.multiple_of(j * TK, TK)
        s = lax.dot_general(q, k_ref[pl.ds(start, TK), :], _NT, preferred_element_type=F32)
        m_new = jnp.maximum(m, jnp.max(s, axis=-1, keepdims=True))
        p = jnp.exp2(s - m_new).astype(BF16)
        acc = jnp.exp2(m - m_new) * acc + jnp.dot(p, v_ref[pl.ds(start, TK), :], preferred_element_type=F32)
        return m_new, acc

    init = (jnp.full((tq, 1), -jnp.inf, F32), jnp.zeros((tq, V_PAD), F32))
    _, acc = lax.fori_loop(0, s_len // TK, body, init)
    o_ref[...] = (acc[:, 0:C_V] / acc[:, C_V:C_V + 1]).astype(o_ref.dtype)


def _flash(q, k, v, bounded):
    bsz, heads, s_len, _ = q.shape
    nq = s_len // TQ

    def call(body, name):
        return pl.pallas_call(
            body,
            grid=(bsz, heads, nq),
            in_specs=[
                pl.BlockSpec((None, None, TQ, QK_PAD), lambda b, hd, i: (b, hd, i, 0)),
                pl.BlockSpec((None, None, s_len, QK_PAD), lambda b, hd, i: (b, hd, 0, 0)),
                pl.BlockSpec((None, None, s_len, V_PAD), lambda b, hd, i: (b, hd, 0, 0)),
            ],
            out_specs=pl.BlockSpec((TQ, C_V), lambda b, hd, i: (b * nq + i, hd)),
            out_shape=jax.ShapeDtypeStruct((bsz * s_len, C_WIDTH), BF16),
            compiler_params=pltpu.CompilerParams(
                dimension_semantics=("arbitrary", "arbitrary", "arbitrary"), vmem_limit_bytes=VMEM_LIMIT),
            name=name,
        )(q, k, v)

    return lax.cond(bounded,
                    lambda: call(_flash_bounded_kernel, "flash_bounded"),
                    lambda: call(_flash_online_kernel, "flash_online"))


def _mix_kernel(au_ref, av_ref, az_ref, bb_ref, bc_ref, bh_ref, bz_ref,
                bcp_ref, bhp_ref, bcn_ref, bhn_ref, cz_ref, o_ref, h_ref, p_ref,
                gsgu_ref, ws_ref, bs_ref, cw_ref, cb_ref, gout_ref, wout_ref, gple_ref, wgate_ref, wpp_ref,
                out_ref, ya_ref, y_ref, *, s_len):
    tm = h_ref.shape[0]
    i = pl.program_id(0)

    for c in range(tm // CHUNK):
        rows = slice(c * CHUNK, (c + 1) * CHUNK)
        for hd in range(A_HEADS):
            cols = slice(hd * A_HEAD, (hd + 1) * A_HEAD)
            vn = _rms(av_ref[rows, cols].astype(F32), gsgu_ref[hd:hd + 1, :]).astype(BF16)
            s = jnp.dot(ws_ref[hd], vn, preferred_element_type=F32) + bs_ref[hd]
            ya_ref[rows, cols] = au_ref[rows, cols].astype(F32) * s * _silu(az_ref[rows, cols].astype(F32))
    y_ref[:, 0:A_WIDTH] = _rms(ya_ref[...], gout_ref[:, 0:A_WIDTH]).astype(BF16)

    xc = bc_ref[...].astype(F32) * bh_ref[...].astype(F32)
    row = lax.broadcasted_iota(jnp.int32, xc.shape, 0)
    has_prev = ((i * tm) % s_len != 0).astype(F32)
    has_next = (((i + 1) * tm) % s_len != 0).astype(F32)
    prev_row = bcp_ref[HALO - 1:HALO, :].astype(F32) * bhp_ref[HALO - 1:HALO, :].astype(F32) * has_prev
    next_row = bcn_ref[0:1, :].astype(F32) * bhn_ref[0:1, :].astype(F32) * has_next
    x_prev = jnp.where(row == 0, prev_row, pltpu.roll(xc, 1, 0))
    x_next = jnp.where(row == tm - 1, next_row, pltpu.roll(xc, tm - 1, 0))
    conv = cb_ref[...] + x_prev * cw_ref[0:1, :] + xc * cw_ref[1:2, :] + x_next * cw_ref[2:3, :]
    yb = bb_ref[...].astype(F32) * conv * _silu(bz_ref[...].astype(F32))
    y_ref[:, A_WIDTH:A_WIDTH + B_WIDTH] = _rms(yb, gout_ref[:, A_WIDTH:A_WIDTH + B_WIDTH]).astype(BF16)

    yc = o_ref[...].astype(F32) * _silu(cz_ref[...].astype(F32))
    y_ref[:, A_WIDTH + B_WIDTH:] = _rms(yc, gout_ref[:, A_WIDTH + B_WIDTH:]).astype(BF16)

    hm = h_ref[...] + jnp.dot(y_ref[...], wout_ref[...], preferred_element_type=F32)
    hn = _rms(hm, gple_ref[...]).astype(BF16)
    gate = jax.nn.sigmoid(jnp.dot(hn, wgate_ref[...], preferred_element_type=F32))
    pp = jnp.dot(p_ref[...].astype(BF16), wpp_ref[...], preferred_element_type=F32)
    out_ref[...] = hm + gate * pp


def _mix_out(proj, o, h, p, gsgu, ws, bs, cw, cb, gout, wout, gple, wgate, wpp, s_len):
    m = h.shape[0]
    tm = TM_MIX
    nhalo = tm // HALO
    last_halo = m // HALO - 1
    row = lambda width, col: pl.BlockSpec((tm, width), lambda i: (i, col // width))
    prev = lambda col: pl.BlockSpec((HALO, B_WIDTH), lambda i: (jnp.maximum(i * nhalo - 1, 0), col // B_WIDTH))
    nxt = lambda col: pl.BlockSpec((HALO, B_WIDTH), lambda i: (jnp.minimum((i + 1) * nhalo, last_halo), col // B_WIDTH))
    const = lambda shape: pl.BlockSpec(shape, lambda i: (0,) * len(shape))
    return pl.pallas_call(
        functools.partial(_mix_kernel, s_len=s_len),
        grid=(m // tm,),
        in_specs=[
            row(A_WIDTH, COL_AU), row(A_WIDTH, COL_AV), row(A_WIDTH, COL_AZ),
            row(B_WIDTH, COL_BB), row(B_WIDTH, COL_BC), row(B_WIDTH, COL_BH), row(B_WIDTH, COL_BZ),
            prev(COL_BC), prev(COL_BH), nxt(COL_BC), nxt(COL_BH),
            row(C_WIDTH, COL_CZ),
            pl.BlockSpec((tm, C_WIDTH), lambda i: (i, 0)),
            pl.BlockSpec((tm, D_MODEL), lambda i: (i, 0)),
            pl.BlockSpec((tm, PLE_DIM), lambda i: (i, 0)),
            const((A_HEADS, A_HEAD)), const((A_HEADS, CHUNK, CHUNK)), const((A_HEADS, CHUNK, A_HEAD)),
            const((CONV_W, B_WIDTH)), const((1, B_WIDTH)), const((1, D_MODEL)),
            const((D_MODEL, D_MODEL)), const((1, D_MODEL)), const((D_MODEL, D_MODEL)), const((PLE_DIM, D_MODEL)),
        ],
        out_specs=pl.BlockSpec((tm, D_MODEL), lambda i: (i, 0)),
        out_shape=jax.ShapeDtypeStruct((m, D_MODEL), F32),
        scratch_shapes=[pltpu.VMEM((tm, A_WIDTH), F32), pltpu.VMEM((tm, D_MODEL), BF16)],
        compiler_params=pltpu.CompilerParams(
            dimension_semantics=("arbitrary",), vmem_limit_bytes=VMEM_LIMIT),
        name="mix_out",
    )(proj, proj, proj, proj, proj, proj, proj, proj, proj, proj, proj, proj, o, h, p,
      gsgu, ws, bs, cw, cb, gout, wout, gple, wgate, wpp)


def _permute_w_in(w):
    a0, b0 = 0, 3 * A_WIDTH
    q0 = b0 + 4 * B_WIDTH
    ckv0 = q0 + C_HEADS * (C_NOPE + C_ROPE)
    kr0 = ckv0 + KV_RANK
    cz0 = kr0 + C_ROPE
    q = w[:, q0:ckv0].reshape(D_MODEL, C_HEADS, C_NOPE + C_ROPE)
    qn = q[:, :, :C_NOPE].reshape(D_MODEL, C_HEADS * C_NOPE)
    qr = q[:, :, C_NOPE:].reshape(D_MODEL, C_HEADS * C_ROPE)
    kr = w[:, kr0:cz0]
    pieces = [qn, w[:, cz0:cz0 + C_WIDTH], qr, w[:, ckv0:kr0], w[:, a0:b0], w[:, b0:q0], kr, kr,
              jnp.zeros((D_MODEL, IN_PAD - IN_USED), w.dtype)]
    return jnp.concatenate(pieces, axis=1).astype(BF16)


def _permute_w_ukv(w):
    kv = w.reshape(KV_RANK, C_HEADS, C_NOPE + C_V)
    kn = kv[:, :, :C_NOPE].reshape(KV_RANK, C_HEADS * C_NOPE)
    v = kv[:, :, C_NOPE:].reshape(KV_RANK, C_HEADS * C_V)
    return jnp.concatenate([kn, v], axis=1).astype(BF16)


def _score_shift(gqn, gqr, gkn, gkr):
    amax = lambda g: jnp.max(jnp.abs(g))
    bound = Q_SCALE * (C_NOPE * amax(gqn) * amax(gkn) + C_ROPE * amax(gqr) * amax(gkr))
    shift = jnp.ceil(bound * 1.02)
    bounded = shift <= MAX_SHIFT
    return jnp.where(bounded, shift, 0.0), bounded


def kernel(x, p, positions, attn_norm, w_in, sgu_norm, w_spatial, b_spatial, conv_w, conv_b, kv_norm, w_ukv,
           q_nope_norm, q_rope_norm, k_nope_norm, k_rope_norm, out_norm, w_out, ple_norm, w_ple_gate, w_ple_proj):
    bsz, s_len, d = x.shape
    depth = w_in.shape[0]
    m = bsz * s_len

    inv = 1.0 / (ROPE_BASE ** (jnp.arange(0, C_ROPE, 2, dtype=F32) / C_ROPE))
    ang = positions.astype(F32)[..., None] * inv
    cos, sin = jnp.cos(ang), jnp.sin(ang)
    cos128 = jnp.tile(cos, (1, 1, 4)).reshape(m, 128)
    sin128 = jnp.tile(jnp.concatenate([-sin, sin], axis=-1), (1, 1, 2)).reshape(m, 128)

    h = x.reshape(m, d)
    for i in range(depth):
        proj = _in_proj(h, attn_norm[i][None, :], _permute_w_in(w_in[i]))
        shift, bounded = _score_shift(q_nope_norm[i], q_rope_norm[i], k_nope_norm[i], k_rope_norm[i])
        kpad = jnp.zeros((1, 128), F32).at[0, C_ROPE].set(-shift)
        q, k, v = _attn_prep(
            proj, cos128, sin128, q_nope_norm[i][None, :], jnp.tile(q_rope_norm[i], 2)[None, :],
            kv_norm[i][None, :], _permute_w_ukv(w_ukv[i]), k_nope_norm[i][None, :],
            jnp.tile(k_rope_norm[i], 2)[None, :], kpad, bsz, s_len)
        o = _flash(q, k, v, bounded)
        bs_full = jnp.broadcast_to(b_spatial[i][:, :, None], (A_HEADS, CHUNK, A_HEAD))
        h = _mix_out(
            proj, o, h, p[i].reshape(m, PLE_DIM), sgu_norm[i], w_spatial[i].astype(BF16), bs_full,
            conv_w[i], conv_b[i][None, :], out_norm[i][None, :], w_out[i].astype(BF16), ple_norm[i][None, :],
            w_ple_gate[i].astype(BF16), w_ple_proj[i].astype(BF16), s_len)
    return h.reshape(bsz, s_len, d)
```

```python
import functools
import math

import jax
import jax.numpy as jnp
from jax import lax
from jax.experimental import pallas as pl
from jax.experimental.pallas import tpu as pltpu

F32 = jnp.float32
BF16 = jnp.bfloat16

D_MODEL = 2048
PLE_DIM = 256
EPS = 1e-6
CHUNK = 128
A_HEAD = 128
A_WIDTH = 512
A_HEADS = 4
B_WIDTH = 512
CONV_W = 3
C_WIDTH = 1024
C_V = 128
C_HEADS = 8
C_NOPE = 128
C_ROPE = 64
KV_RANK = 512
ROPE_BASE = 10000.0
QK_PAD = 256
V_PAD = 256
MAX_SHIFT = 50.0

COL_QN = 0
COL_CZ = 1024
COL_QR = 2048
COL_CKV = 2560
COL_AU, COL_AV, COL_AZ = 3072, 3584, 4096
COL_BB, COL_BC, COL_BH, COL_BZ = 4608, 5120, 5632, 6144
COL_KR = 6656
IN_USED = 6784
IN_PAD = 6912

Q_SCALE = (C_NOPE + C_ROPE) ** -0.5 * math.log2(math.e)

VMEM_LIMIT = 56 * 1024 * 1024

TM_IN, TN_IN = 1024, 768
TM_PREP = 512
TQ, TK = 1024, 256
TM_MIX = 256
HALO = 16


def _rms(x, gain):
    ms = jnp.mean(x * x, axis=-1, keepdims=True)
    return x * lax.rsqrt(ms + EPS) * gain


def _silu(z):
    return z * jax.nn.sigmoid(z)


def _layer_spec(shape):
    return pl.BlockSpec((None,) + shape, lambda *a: (a[-1][0],) + (0,) * len(shape))


def _params(semantics):
    return pltpu.CompilerParams(dimension_semantics=semantics, vmem_limit_bytes=VMEM_LIMIT)


def _in_proj_kernel(lid_ref, x_ref, g_ref, w_ref, o_ref, hn_ref):
    @pl.when(pl.program_id(1) == 0)
    def _():
        hn_ref[...] = _rms(x_ref[...], g_ref[...]).astype(BF16)

    o_ref[...] = jnp.dot(hn_ref[...], w_ref[...], preferred_element_type=F32).astype(o_ref.dtype)


def _in_proj(lid, h, gain, w):
    m = h.shape[0]
    return pl.pallas_call(
        _in_proj_kernel,
        grid_spec=pltpu.PrefetchScalarGridSpec(
            num_scalar_prefetch=1,
            grid=(m // TM_IN, IN_PAD // TN_IN),
            in_specs=[
                pl.BlockSpec((TM_IN, D_MODEL), lambda i, j, l: (i, 0)),
                _layer_spec((1, D_MODEL)),
                pl.BlockSpec((None, D_MODEL, TN_IN), lambda i, j, l: (l[0], 0, j)),
            ],
            out_specs=pl.BlockSpec((TM_IN, TN_IN), lambda i, j, l: (i, j)),
            scratch_shapes=[pltpu.VMEM((TM_IN, D_MODEL), BF16)],
        ),
        out_shape=jax.ShapeDtypeStruct((m, IN_PAD), BF16),
        compiler_params=_params(("arbitrary", "arbitrary")),
        name="in_proj",
    )(lid, h, gain, w)


def _prep_kernel(lid_ref, qn_ref, qr_ref, ckv_ref, kr_ref, cos_ref, sin_ref, gqn_ref, gqr_ref, gkv_ref,
                 wukv_ref, gkn_ref, gkr_ref, kpad_ref, q_out, k_out, v_out):
    tm = qn_ref.shape[0]
    cos = cos_ref[...]
    sin = sin_ref[...]
    lane = lax.broadcasted_iota(jnp.int32, (tm, 128), 1)
    low_half = (lane & 32) == 0
    first = lane < 64

    def rope(x):
        swapped = jnp.where(low_half, pltpu.roll(x, 96, 1), pltpu.roll(x, 32, 1))
        return x * cos + swapped * sin

    for hd in range(C_HEADS):
        x = qn_ref[:, hd * C_NOPE:(hd + 1) * C_NOPE].astype(F32)
        q_out[hd, :, 0:C_NOPE] = (_rms(x, gqn_ref[...]) * Q_SCALE).astype(BF16)

    q_pad = jnp.where(lane == C_ROPE, 1.0, 0.0)
    for j in range(C_HEADS // 2):
        x = qr_ref[:, j * 128:(j + 1) * 128].astype(F32)
        x2 = x * x
        s_lo = jnp.sum(jnp.where(first, x2, 0.0), axis=-1, keepdims=True)
        s_hi = jnp.sum(jnp.where(first, 0.0, x2), axis=-1, keepdims=True)
        ms = jnp.where(first, s_lo, s_hi) * (1.0 / C_ROPE)
        r = rope(x * lax.rsqrt(ms + EPS) * gqr_ref[...]) * Q_SCALE
        q_out[2 * j, :, C_NOPE:QK_PAD] = jnp.where(first, r, q_pad).astype(BF16)
        q_out[2 * j + 1, :, C_NOPE:QK_PAD] = jnp.where(first, pltpu.roll(r, 64, 1), q_pad).astype(BF16)

    cn = _rms(ckv_ref[...].astype(F32), gkv_ref[...]).astype(BF16)
    kv = jnp.dot(cn, wukv_ref[...], preferred_element_type=F32)
    kr = rope(_rms(kr_ref[...].astype(F32), gkr_ref[...]))
    kr = jnp.where(first, kr, kpad_ref[...]).astype(BF16)
    ones_col = jnp.where(lane == 0, 1.0, 0.0).astype(BF16)
    for hd in range(C_HEADS):
        kn = kv[:, hd * C_NOPE:(hd + 1) * C_NOPE]
        k_out[hd, :, 0:C_NOPE] = _rms(kn, gkn_ref[...]).astype(BF16)
        k_out[hd, :, C_NOPE:QK_PAD] = kr
        v_out[hd, :, 0:C_V] = kv[:, C_WIDTH + hd * C_V:C_WIDTH + (hd + 1) * C_V].astype(BF16)
        v_out[hd, :, C_V:V_PAD] = ones_col


def _attn_prep(lid, proj, cos128, sin128, gqn, gqr, gkv, wukv, gkn, gkr, kpad, bsz, s_len):
    tm = TM_PREP
    nst = s_len // tm
    row = lambda width, col: pl.BlockSpec((tm, width), lambda i, l: (i, col // width))
    head_out = lambda width: pl.BlockSpec((None, C_HEADS, tm, width), lambda i, l: (i // nst, 0, i % nst, 0))
    head_shape = lambda width: jax.ShapeDtypeStruct((bsz, C_HEADS, s_len, width), BF16)
    return pl.pallas_call(
        _prep_kernel,
        grid_spec=pltpu.PrefetchScalarGridSpec(
            num_scalar_prefetch=1,
            grid=(bsz * nst,),
            in_specs=[
                row(C_WIDTH, COL_QN), row(C_HEADS * C_ROPE, COL_QR), row(KV_RANK, COL_CKV), row(128, COL_KR),
                pl.BlockSpec((tm, 128), lambda i, l: (i, 0)), pl.BlockSpec((tm, 128), lambda i, l: (i, 0)),
                _layer_spec((1, C_NOPE)), _layer_spec((1, 128)), _layer_spec((1, KV_RANK)),
                _layer_spec((KV_RANK, 2 * C_WIDTH)), _layer_spec((1, C_NOPE)), _layer_spec((1, 128)),
                _layer_spec((1, 128)),
            ],
            out_specs=[head_out(QK_PAD), head_out(QK_PAD), head_out(V_PAD)],
        ),
        out_shape=[head_shape(QK_PAD), head_shape(QK_PAD), head_shape(V_PAD)],
        compiler_params=_params(("arbitrary",)),
        name="attn_prep",
    )(lid, proj, proj, proj, proj, cos128, sin128, gqn, gqr, gkv, wukv, gkn, gkr, kpad)


_NT = (((1,), (1,)), ((), ()))


def _flash_bounded_kernel(q_ref, k_ref, v_ref, o_ref):
    tq = q_ref.shape[0]
    s_len = k_ref.shape[0]
    q = q_ref[...]
    nk = s_len // TK

    def probs(j):
        s = lax.dot_general(q, k_ref[j * TK:(j + 1) * TK, :], _NT, preferred_element_type=F32)
        return jnp.exp2(s).astype(BF16)

    p = probs(0)
    acc = jnp.zeros((tq, V_PAD), F32)
    for j in range(nk):
        p_next = probs(j + 1) if j + 1 < nk else None
        acc = acc + jnp.dot(p, v_ref[j * TK:(j + 1) * TK, :], preferred_element_type=F32)
        p = p_next
    o_ref[...] = (acc[:, 0:C_V] / acc[:, C_V:C_V + 1]).astype(o_ref.dtype)


def _flash_online_kernel(q_ref, k_ref, v_ref, o_ref):
    tq = q_ref.shape[0]
    s_len = k_ref.shape[0]
    q = q_ref[...]

    def body(j, carry):
        m, acc = carry
        start = pl.multiple_of(j * TK, TK)
        s = lax.dot_general(q, k_ref[pl.ds(start, TK), :], _NT, preferred_element_type=F32)
        m_new = jnp.maximum(m, jnp.max(s, axis=-1, keepdims=True))
        p = jnp.exp2(s - m_new).astype(BF16)
        acc = jnp.exp2(m - m_new) * acc + jnp.dot(p, v_ref[pl.ds(start, TK), :], preferred_element_type=F32)
        return m_new, acc

    init = (jnp.full((tq, 1), -jnp.inf, F32), jnp.zeros((tq, V_PAD), F32))
    _, acc = lax.fori_loop(0, s_len // TK, body, init)
    o_ref[...] = (acc[:, 0:C_V] / acc[:, C_V:C_V + 1]).astype(o_ref.dtype)


def _flash(q, k, v, bounded):
    bsz, heads, s_len, _ = q.shape
    nq = s_len // TQ

    def call(body, name):
        return pl.pallas_call(
            body,
            grid=(bsz, heads, nq),
            in_specs=[
                pl.BlockSpec((None, None, TQ, QK_PAD), lambda b, hd, i: (b, hd, i, 0)),
                pl.BlockSpec((None, None, s_len, QK_PAD), lambda b, hd, i: (b, hd, 0, 0)),
                pl.BlockSpec((None, None, s_len, V_PAD), lambda b, hd, i: (b, hd, 0, 0)),
            ],
            out_specs=pl.BlockSpec((TQ, C_V), lambda b, hd, i: (b * nq + i, hd)),
            out_shape=jax.ShapeDtypeStruct((bsz * s_len, C_WIDTH), BF16),
            compiler_params=_params(("arbitrary", "arbitrary", "arbitrary")),
            name=name,
        )(q, k, v)

    return lax.cond(bounded,
                    lambda: call(_flash_bounded_kernel, "flash_bounded"),
                    lambda: call(_flash_online_kernel, "flash_online"))


def _mix_kernel(lid_ref, au_ref, av_ref, az_ref, bb_ref, bc_ref, bh_ref, bz_ref,
                bcp_ref, bhp_ref, bcn_ref, bhn_ref, cz_ref, o_ref, h_ref, p_ref,
                gsgu_ref, ws_ref, bs_ref, cw_ref, cb_ref, gout_ref, wout_ref, gple_ref, wgate_ref, wpp_ref,
                out_ref, ya_ref, y_ref, *, s_len):
    tm = h_ref.shape[0]
    i = pl.program_id(0)

    for c in range(tm // CHUNK):
        rows = slice(c * CHUNK, (c + 1) * CHUNK)
        for hd in range(A_HEADS):
            cols = slice(hd * A_HEAD, (hd + 1) * A_HEAD)
            vn = _rms(av_ref[rows, cols].astype(F32), gsgu_ref[hd:hd + 1, :]).astype(BF16)
            s = jnp.dot(ws_ref[hd], vn, preferred_element_type=F32) + bs_ref[hd]
            ya_ref[rows, cols] = au_ref[rows, cols].astype(F32) * s * _silu(az_ref[rows, cols].astype(F32))
    y_ref[:, 0:A_WIDTH] = _rms(ya_ref[...], gout_ref[:, 0:A_WIDTH]).astype(BF16)

    xc = bc_ref[...].astype(F32) * bh_ref[...].astype(F32)
    row = lax.broadcasted_iota(jnp.int32, xc.shape, 0)
    has_prev = ((i * tm) % s_len != 0).astype(F32)
    has_next = (((i + 1) * tm) % s_len != 0).astype(F32)
    prev_row = bcp_ref[HALO - 1:HALO, :].astype(F32) * bhp_ref[HALO - 1:HALO, :].astype(F32) * has_prev
    next_row = bcn_ref[0:1, :].astype(F32) * bhn_ref[0:1, :].astype(F32) * has_next
    x_prev = jnp.where(row == 0, prev_row, pltpu.roll(xc, 1, 0))
    x_next = jnp.where(row == tm - 1, next_row, pltpu.roll(xc, tm - 1, 0))
    conv = cb_ref[...] + x_prev * cw_ref[0:1, :] + xc * cw_ref[1:2, :] + x_next * cw_ref[2:3, :]
    yb = bb_ref[...].astype(F32) * conv * _silu(bz_ref[...].astype(F32))
    y_ref[:, A_WIDTH:A_WIDTH + B_WIDTH] = _rms(yb, gout_ref[:, A_WIDTH:A_WIDTH + B_WIDTH]).astype(BF16)

    yc = o_ref[...].astype(F32) * _silu(cz_ref[...].astype(F32))
    y_ref[:, A_WIDTH + B_WIDTH:] = _rms(yc, gout_ref[:, A_WIDTH + B_WIDTH:]).astype(BF16)

    hm = h_ref[...] + jnp.dot(y_ref[...], wout_ref[...], preferred_element_type=F32)
    hn = _rms(hm, gple_ref[...]).astype(BF16)
    gate = jax.nn.sigmoid(jnp.dot(hn, wgate_ref[...], preferred_element_type=F32))
    pp = jnp.dot(p_ref[...].astype(BF16), wpp_ref[...], preferred_element_type=F32)
    out_ref[...] = hm + gate * pp


def _mix_out(lid, proj, o, h, p, gsgu, ws, bs, cw, cb, gout, wout, gple, wgate, wpp, s_len):
    m = h.shape[0]
    tm = TM_MIX
    nhalo = tm // HALO
    last_halo = m // HALO - 1
    row = lambda width, col: pl.BlockSpec((tm, width), lambda i, l: (i, col // width))
    prev = lambda col: pl.BlockSpec(
        (HALO, B_WIDTH), lambda i, l: (jnp.maximum(i * nhalo - 1, 0), col // B_WIDTH))
    nxt = lambda col: pl.BlockSpec(
        (HALO, B_WIDTH), lambda i, l: (jnp.minimum((i + 1) * nhalo, last_halo), col // B_WIDTH))
    return pl.pallas_call(
        functools.partial(_mix_kernel, s_len=s_len),
        grid_spec=pltpu.PrefetchScalarGridSpec(
            num_scalar_prefetch=1,
            grid=(m // tm,),
            in_specs=[
                row(A_WIDTH, COL_AU), row(A_WIDTH, COL_AV), row(A_WIDTH, COL_AZ),
                row(B_WIDTH, COL_BB), row(B_WIDTH, COL_BC), row(B_WIDTH, COL_BH), row(B_WIDTH, COL_BZ),
                prev(COL_BC), prev(COL_BH), nxt(COL_BC), nxt(COL_BH),
                row(C_WIDTH, COL_CZ),
                pl.BlockSpec((tm, C_WIDTH), lambda i, l: (i, 0)),
                pl.BlockSpec((tm, D_MODEL), lambda i, l: (i, 0)),
                pl.BlockSpec((None, tm, PLE_DIM), lambda i, l: (l[0], i, 0)),
                _layer_spec((A_HEADS, A_HEAD)), _layer_spec((A_HEADS, CHUNK, CHUNK)),
                _layer_spec((A_HEADS, CHUNK, A_HEAD)),
                _layer_spec((CONV_W, B_WIDTH)), _layer_spec((1, B_WIDTH)), _layer_spec((1, D_MODEL)),
                _layer_spec((D_MODEL, D_MODEL)), _layer_spec((1, D_MODEL)), _layer_spec((D_MODEL, D_MODEL)),
                _layer_spec((PLE_DIM, D_MODEL)),
            ],
            out_specs=pl.BlockSpec((tm, D_MODEL), lambda i, l: (i, 0)),
            scratch_shapes=[pltpu.VMEM((tm, A_WIDTH), F32), pltpu.VMEM((tm, D_MODEL), BF16)],
        ),
        out_shape=jax.ShapeDtypeStruct((m, D_MODEL), F32),
        compiler_params=_params(("arbitrary",)),
        name="mix_out",
    )(lid, proj, proj, proj, proj, proj, proj, proj, proj, proj, proj, proj, proj, o, h, p,
      gsgu, ws, bs, cw, cb, gout, wout, gple, wgate, wpp)


def _permute_w_in(w):
    depth = w.shape[0]
    a0, b0 = 0, 3 * A_WIDTH
    q0 = b0 + 4 * B_WIDTH
    ckv0 = q0 + C_HEADS * (C_NOPE + C_ROPE)
    kr0 = ckv0 + KV_RANK
    cz0 = kr0 + C_ROPE
    w = w.astype(BF16)
    q = w[:, :, q0:ckv0].reshape(depth, D_MODEL, C_HEADS, C_NOPE + C_ROPE)
    qn = q[..., :C_NOPE].reshape(depth, D_MODEL, C_HEADS * C_NOPE)
    qr = q[..., C_NOPE:].reshape(depth, D_MODEL, C_HEADS * C_ROPE)
    kr = w[:, :, kr0:cz0]
    pieces = [qn, w[:, :, cz0:cz0 + C_WIDTH], qr, w[:, :, ckv0:kr0], w[:, :, a0:b0], w[:, :, b0:q0], kr, kr,
              jnp.zeros((depth, D_MODEL, IN_PAD - IN_USED), BF16)]
    return jnp.concatenate(pieces, axis=2)


def _permute_w_ukv(w):
    depth = w.shape[0]
    kv = w.astype(BF16).reshape(depth, KV_RANK, C_HEADS, C_NOPE + C_V)
    kn = kv[..., :C_NOPE].reshape(depth, KV_RANK, C_HEADS * C_NOPE)
    v = kv[..., C_NOPE:].reshape(depth, KV_RANK, C_HEADS * C_V)
    return jnp.concatenate([kn, v], axis=2)


def _score_shift(gqn, gqr, gkn, gkr):
    amax = lambda g: jnp.max(jnp.abs(g), axis=-1)
    bound = Q_SCALE * (C_NOPE * amax(gqn) * amax(gkn) + C_ROPE * amax(gqr) * amax(gkr))
    shift = jnp.ceil(bound * 1.02)
    bounded = shift <= MAX_SHIFT
    return jnp.where(bounded, shift, 0.0), bounded


def kernel(x, p, positions, attn_norm, w_in, sgu_norm, w_spatial, b_spatial, conv_w, conv_b, kv_norm, w_ukv,
           q_nope_norm, q_rope_norm, k_nope_norm, k_rope_norm, out_norm, w_out, ple_norm, w_ple_gate, w_ple_proj):
    bsz, s_len, d = x.shape
    depth = w_in.shape[0]
    m = bsz * s_len

    inv = 1.0 / (ROPE_BASE ** (jnp.arange(0, C_ROPE, 2, dtype=F32) / C_ROPE))
    ang = positions.astype(F32)[..., None] * inv
    cos, sin = jnp.cos(ang), jnp.sin(ang)
    cos128 = jnp.tile(cos, (1, 1, 4)).reshape(m, 128)
    sin128 = jnp.tile(jnp.concatenate([-sin, sin], axis=-1), (1, 1, 2)).reshape(m, 128)

    vec = lambda g: g[:, None, :]
    w_in_p = _permute_w_in(w_in)
    w_ukv_p = _permute_w_ukv(w_ukv)
    shift, bounded = _score_shift(q_nope_norm, q_rope_norm, k_nope_norm, k_rope_norm)
    kpad = jnp.zeros((depth, 1, 128), F32).at[:, 0, C_ROPE].set(-shift)
    gqr2 = vec(jnp.tile(q_rope_norm, (1, 2)))
    gkr2 = vec(jnp.tile(k_rope_norm, (1, 2)))
    bs_full = jnp.broadcast_to(b_spatial[..., None], (depth, A_HEADS, CHUNK, A_HEAD))
    ws_b, wout_b = w_spatial.astype(BF16), w_out.astype(BF16)
    wgate_b, wpp_b = w_ple_gate.astype(BF16), w_ple_proj.astype(BF16)
    p2 = p.reshape(depth, m, PLE_DIM)

    h = x.reshape(m, d)
    for i in range(depth):
        lid = jnp.full((1,), i, jnp.int32)
        proj = _in_proj(lid, h, vec(attn_norm), w_in_p)
        q, k, v = _attn_prep(lid, proj, cos128, sin128, vec(q_nope_norm), gqr2, vec(kv_norm), w_ukv_p,
                             vec(k_nope_norm), gkr2, kpad, bsz, s_len)
        o = _flash(q, k, v, bounded[i])
        h = _mix_out(lid, proj, o, h, p2, sgu_norm, ws_b, bs_full, conv_w, vec(conv_b), vec(out_norm),
                     wout_b, vec(ple_norm), wgate_b, wpp_b, s_len)
    return h.reshape(bsz, s_len, d)
```

```python
import functools
import math

import jax
import jax.numpy as jnp
from jax import lax
from jax.experimental import pallas as pl
from jax.experimental.pallas import tpu as pltpu

F32 = jnp.float32
BF16 = jnp.bfloat16

D_MODEL = 2048
PLE_DIM = 256
EPS = 1e-6
CHUNK = 128
A_HEAD = 128
A_WIDTH = 512
A_HEADS = 4
B_WIDTH = 512
CONV_W = 3
C_WIDTH = 1024
C_V = 128
C_HEADS = 8
C_NOPE = 128
C_ROPE = 64
KV_RANK = 512
ROPE_BASE = 10000.0
QK_PAD = 256
V_PAD = 256
MAX_SHIFT = 50.0

COL_QN = 0
COL_CZ = 1024
COL_QR = 2048
COL_CKV = 2560
COL_AU, COL_AV, COL_AZ = 3072, 3584, 4096
COL_BB, COL_BC, COL_BH, COL_BZ = 4608, 5120, 5632, 6144
COL_KR = 6656
IN_USED = 6784
IN_PAD = 6912

Q_SCALE = (C_NOPE + C_ROPE) ** -0.5 * math.log2(math.e)

VMEM_LIMIT = 56 * 1024 * 1024

TM_IN, TN_IN = 512, 768
VMEM_LIMIT_IN = 60 * 1024 * 1024
TM_PREP = 512
TM_WPREP = 256
TQ, TK = 1024, 256
TM_MIX = 256
HALO = 16


def _rms(x, gain):
    ms = jnp.mean(x * x, axis=-1, keepdims=True)
    return x * lax.rsqrt(ms + EPS) * gain


def _silu(z):
    return z * jax.nn.sigmoid(z)


def _layer_spec(shape):
    return pl.BlockSpec((None,) + shape, lambda *a: (a[-1][0],) + (0,) * len(shape))


def _params(semantics):
    return pltpu.CompilerParams(dimension_semantics=semantics, vmem_limit_bytes=VMEM_LIMIT)


def _in_proj_kernel(lid_ref, x_ref, g_ref, w_ref, o_ref, hn_ref):
    hn_ref[...] = _rms(x_ref[...], g_ref[...]).astype(BF16)
    for j in range(IN_PAD // TN_IN):
        cols = slice(j * TN_IN, (j + 1) * TN_IN)
        o_ref[:, cols] = jnp.dot(hn_ref[...], w_ref[:, cols], preferred_element_type=F32).astype(o_ref.dtype)


def _in_proj(lid, h, gain, w):
    m = h.shape[0]
    return pl.pallas_call(
        _in_proj_kernel,
        grid_spec=pltpu.PrefetchScalarGridSpec(
            num_scalar_prefetch=1,
            grid=(m // TM_IN,),
            in_specs=[
                pl.BlockSpec((TM_IN, D_MODEL), lambda i, l: (i, 0)),
                _layer_spec((1, D_MODEL)),
                pl.BlockSpec((None, D_MODEL, IN_PAD), lambda i, l: (l[0], 0, 0), pipeline_mode=pl.Buffered(1)),
            ],
            out_specs=pl.BlockSpec((TM_IN, IN_PAD), lambda i, l: (i, 0)),
            scratch_shapes=[pltpu.VMEM((TM_IN, D_MODEL), BF16)],
        ),
        out_shape=jax.ShapeDtypeStruct((m, IN_PAD), BF16),
        compiler_params=pltpu.CompilerParams(dimension_semantics=("arbitrary",), vmem_limit_bytes=VMEM_LIMIT_IN),
        name="in_proj",
    )(lid, h, gain, w)


def _prep_kernel(lid_ref, qn_ref, qr_ref, ckv_ref, kr_ref, cos_ref, sin_ref, gqn_ref, gqr_ref, gkv_ref,
                 wukv_ref, gkn_ref, gkr_ref, kpad_ref, q_out, k_out, v_out):
    tm = qn_ref.shape[0]
    cos = cos_ref[...]
    sin = sin_ref[...]
    lane = lax.broadcasted_iota(jnp.int32, (tm, 128), 1)
    low_half = (lane & 32) == 0
    first = lane < 64

    def rope(x):
        swapped = jnp.where(low_half, pltpu.roll(x, 96, 1), pltpu.roll(x, 32, 1))
        return x * cos + swapped * sin

    for hd in range(C_HEADS):
        x = qn_ref[:, hd * C_NOPE:(hd + 1) * C_NOPE].astype(F32)
        q_out[hd, :, 0:C_NOPE] = (_rms(x, gqn_ref[...]) * Q_SCALE).astype(BF16)

    q_pad = jnp.where(lane == C_ROPE, 1.0, 0.0)
    for j in range(C_HEADS // 2):
        x = qr_ref[:, j * 128:(j + 1) * 128].astype(F32)
        x2 = x * x
        s_lo = jnp.sum(jnp.where(first, x2, 0.0), axis=-1, keepdims=True)
        s_hi = jnp.sum(jnp.where(first, 0.0, x2), axis=-1, keepdims=True)
        ms = jnp.where(first, s_lo, s_hi) * (1.0 / C_ROPE)
        r = rope(x * lax.rsqrt(ms + EPS) * gqr_ref[...]) * Q_SCALE
        q_out[2 * j, :, C_NOPE:QK_PAD] = jnp.where(first, r, q_pad).astype(BF16)
        q_out[2 * j + 1, :, C_NOPE:QK_PAD] = jnp.where(first, pltpu.roll(r, 64, 1), q_pad).astype(BF16)

    cn = _rms(ckv_ref[...].astype(F32), gkv_ref[...]).astype(BF16)
    kv = jnp.dot(cn, wukv_ref[...], preferred_element_type=F32)
    kr = rope(_rms(kr_ref[...].astype(F32), gkr_ref[...]))
    kr = jnp.where(first, kr, kpad_ref[...]).astype(BF16)
    ones_col = jnp.where(lane == 0, 1.0, 0.0).astype(BF16)
    for hd in range(C_HEADS):
        kn = kv[:, hd * C_NOPE:(hd + 1) * C_NOPE]
        k_out[hd, :, 0:C_NOPE] = _rms(kn, gkn_ref[...]).astype(BF16)
        k_out[hd, :, C_NOPE:QK_PAD] = kr
        v_out[hd, :, 0:C_V] = kv[:, C_WIDTH + hd * C_V:C_WIDTH + (hd + 1) * C_V].astype(BF16)
        v_out[hd, :, C_V:V_PAD] = ones_col


def _attn_prep(lid, proj, cos128, sin128, gqn, gqr, gkv, wukv, gkn, gkr, kpad, bsz, s_len):
    tm = TM_PREP
    nst = s_len // tm
    row = lambda width, col: pl.BlockSpec((tm, width), lambda i, l: (i, col // width))
    head_out = lambda width: pl.BlockSpec((None, C_HEADS, tm, width), lambda i, l: (i // nst, 0, i % nst, 0))
    head_shape = lambda width: jax.ShapeDtypeStruct((bsz, C_HEADS, s_len, width), BF16)
    return pl.pallas_call(
        _prep_kernel,
        grid_spec=pltpu.PrefetchScalarGridSpec(
            num_scalar_prefetch=1,
            grid=(bsz * nst,),
            in_specs=[
                row(C_WIDTH, COL_QN), row(C_HEADS * C_ROPE, COL_QR), row(KV_RANK, COL_CKV), row(128, COL_KR),
                pl.BlockSpec((tm, 128), lambda i, l: (i, 0)), pl.BlockSpec((tm, 128), lambda i, l: (i, 0)),
                _layer_spec((1, C_NOPE)), _layer_spec((1, 128)), _layer_spec((1, KV_RANK)),
                _layer_spec((KV_RANK, 2 * C_WIDTH)), _layer_spec((1, C_NOPE)), _layer_spec((1, 128)),
                _layer_spec((1, 128)),
            ],
            out_specs=[head_out(QK_PAD), head_out(QK_PAD), head_out(V_PAD)],
        ),
        out_shape=[head_shape(QK_PAD), head_shape(QK_PAD), head_shape(V_PAD)],
        compiler_params=_params(("arbitrary",)),
        name="attn_prep",
    )(lid, proj, proj, proj, proj, cos128, sin128, gqn, gqr, gkv, wukv, gkn, gkr, kpad)


_NT = (((1,), (1,)), ((), ()))


def _flash_bounded_kernel(q_ref, k_ref, v_ref, o_ref):
    tq = q_ref.shape[0]
    s_len = k_ref.shape[0]
    q = q_ref[...]
    nk = s_len // TK

    def probs(j):
        s = lax.dot_general(q, k_ref[j * TK:(j + 1) * TK, :], _NT, preferred_element_type=F32)
        return jnp.exp2(s).astype(BF16)

    p = probs(0)
    acc = jnp.zeros((tq, V_PAD), F32)
    for j in range(nk):
        p_next = probs(j + 1) if j + 1 < nk else None
        acc = acc + jnp.dot(p, v_ref[j * TK:(j + 1) * TK, :], preferred_element_type=F32)
        p = p_next
    o_ref[...] = (acc[:, 0:C_V] / acc[:, C_V:C_V + 1]).astype(o_ref.dtype)


def _flash_online_kernel(q_ref, k_ref, v_ref, o_ref):
    tq = q_ref.shape[0]
    s_len = k_ref.shape[0]
    q = q_ref[...]

    def body(j, carry):
        m, acc = carry
        start = pl.multiple_of(j * TK, TK)
        s = lax.dot_general(q, k_ref[pl.ds(start, TK), :], _NT, preferred_element_type=F32)
        m_new = jnp.maximum(m, jnp.max(s, axis=-1, keepdims=True))
        p = jnp.exp2(s - m_new).astype(BF16)
        acc = jnp.exp2(m - m_new) * acc + jnp.dot(p, v_ref[pl.ds(start, TK), :], preferred_element_type=F32)
        return m_new, acc

    init = (jnp.full((tq, 1), -jnp.inf, F32), jnp.zeros((tq, V_PAD), F32))
    _, acc = lax.fori_loop(0, s_len // TK, body, init)
    o_ref[...] = (acc[:, 0:C_V] / acc[:, C_V:C_V + 1]).astype(o_ref.dtype)


def _flash(q, k, v, bounded):
    bsz, heads, s_len, _ = q.shape
    nq = s_len // TQ

    def call(body, name):
        return pl.pallas_call(
            body,
            grid=(bsz, heads, nq),
            in_specs=[
                pl.BlockSpec((None, None, TQ, QK_PAD), lambda b, hd, i: (b, hd, i, 0)),
                pl.BlockSpec((None, None, s_len, QK_PAD), lambda b, hd, i: (b, hd, 0, 0)),
                pl.BlockSpec((None, None, s_len, V_PAD), lambda b, hd, i: (b, hd, 0, 0)),
            ],
            out_specs=pl.BlockSpec((TQ, C_V), lambda b, hd, i: (b * nq + i, hd)),
            out_shape=jax.ShapeDtypeStruct((bsz * s_len, C_WIDTH), BF16),
            compiler_params=_params(("arbitrary", "arbitrary", "arbitrary")),
            name=name,
        )(q, k, v)

    return lax.cond(bounded,
                    lambda: call(_flash_bounded_kernel, "flash_bounded"),
                    lambda: call(_flash_online_kernel, "flash_online"))


def _mix_kernel(lid_ref, au_ref, av_ref, az_ref, bb_ref, bc_ref, bh_ref, bz_ref,
                bcp_ref, bhp_ref, bcn_ref, bhn_ref, cz_ref, o_ref, h_ref, p_ref,
                gsgu_ref, ws_ref, bs_ref, cw_ref, cb_ref, gout_ref, wout_ref, gple_ref, wgate_ref, wpp_ref,
                out_ref, ya_ref, y_ref, *, s_len):
    tm = h_ref.shape[0]
    i = pl.program_id(0)

    for c in range(tm // CHUNK):
        rows = slice(c * CHUNK, (c + 1) * CHUNK)
        for hd in range(A_HEADS):
            cols = slice(hd * A_HEAD, (hd + 1) * A_HEAD)
            vn = _rms(av_ref[rows, cols].astype(F32), gsgu_ref[hd:hd + 1, :]).astype(BF16)
            s = jnp.dot(ws_ref[hd], vn, preferred_element_type=F32) + bs_ref[hd]
            ya_ref[rows, cols] = au_ref[rows, cols].astype(F32) * s * _silu(az_ref[rows, cols].astype(F32))
    y_ref[:, 0:A_WIDTH] = _rms(ya_ref[...], gout_ref[:, 0:A_WIDTH]).astype(BF16)

    xc = bc_ref[...].astype(F32) * bh_ref[...].astype(F32)
    row = lax.broadcasted_iota(jnp.int32, xc.shape, 0)
    has_prev = ((i * tm) % s_len != 0).astype(F32)
    has_next = (((i + 1) * tm) % s_len != 0).astype(F32)
    prev_row = bcp_ref[HALO - 1:HALO, :].astype(F32) * bhp_ref[HALO - 1:HALO, :].astype(F32) * has_prev
    next_row = bcn_ref[0:1, :].astype(F32) * bhn_ref[0:1, :].astype(F32) * has_next
    x_prev = jnp.where(row == 0, prev_row, pltpu.roll(xc, 1, 0))
    x_next = jnp.where(row == tm - 1, next_row, pltpu.roll(xc, tm - 1, 0))
    conv = cb_ref[...] + x_prev * cw_ref[0:1, :] + xc * cw_ref[1:2, :] + x_next * cw_ref[2:3, :]
    yb = bb_ref[...].astype(F32) * conv * _silu(bz_ref[...].astype(F32))
    y_ref[:, A_WIDTH:A_WIDTH + B_WIDTH] = _rms(yb, gout_ref[:, A_WIDTH:A_WIDTH + B_WIDTH]).astype(BF16)

    yc = o_ref[...].astype(F32) * _silu(cz_ref[...].astype(F32))
    y_ref[:, A_WIDTH + B_WIDTH:] = _rms(yc, gout_ref[:, A_WIDTH + B_WIDTH:]).astype(BF16)

    hm = h_ref[...] + jnp.dot(y_ref[...], wout_ref[...], preferred_element_type=F32)
    hn = _rms(hm, gple_ref[...]).astype(BF16)
    gate = jax.nn.sigmoid(jnp.dot(hn, wgate_ref[...], preferred_element_type=F32))
    pp = jnp.dot(p_ref[...].astype(BF16), wpp_ref[...], preferred_element_type=F32)
    out_ref[...] = hm + gate * pp


def _mix_out(lid, proj, o, h, p, gsgu, ws, bs, cw, cb, gout, wout, gple, wgate, wpp, s_len):
    m = h.shape[0]
    tm = TM_MIX
    nhalo = tm // HALO
    last_halo = m // HALO - 1
    row = lambda width, col: pl.BlockSpec((tm, width), lambda i, l: (i, col // width))
    prev = lambda col: pl.BlockSpec(
        (HALO, B_WIDTH), lambda i, l: (jnp.maximum(i * nhalo - 1, 0), col // B_WIDTH))
    nxt = lambda col: pl.BlockSpec(
        (HALO, B_WIDTH), lambda i, l: (jnp.minimum((i + 1) * nhalo, last_halo), col // B_WIDTH))
    return pl.pallas_call(
        functools.partial(_mix_kernel, s_len=s_len),
        grid_spec=pltpu.PrefetchScalarGridSpec(
            num_scalar_prefetch=1,
            grid=(m // tm,),
            in_specs=[
                row(A_WIDTH, COL_AU), row(A_WIDTH, COL_AV), row(A_WIDTH, COL_AZ),
                row(B_WIDTH, COL_BB), row(B_WIDTH, COL_BC), row(B_WIDTH, COL_BH), row(B_WIDTH, COL_BZ),
                prev(COL_BC), prev(COL_BH), nxt(COL_BC), nxt(COL_BH),
                row(C_WIDTH, COL_CZ),
                pl.BlockSpec((tm, C_WIDTH), lambda i, l: (i, 0)),
                pl.BlockSpec((tm, D_MODEL), lambda i, l: (i, 0)),
                pl.BlockSpec((None, tm, PLE_DIM), lambda i, l: (l[0], i, 0)),
                _layer_spec((A_HEADS, A_HEAD)), _layer_spec((A_HEADS, CHUNK, CHUNK)),
                _layer_spec((A_HEADS, CHUNK, A_HEAD)),
                _layer_spec((CONV_W, B_WIDTH)), _layer_spec((1, B_WIDTH)), _layer_spec((1, D_MODEL)),
                _layer_spec((D_MODEL, D_MODEL)), _layer_spec((1, D_MODEL)), _layer_spec((D_MODEL, D_MODEL)),
                _layer_spec((PLE_DIM, D_MODEL)),
            ],
            out_specs=pl.BlockSpec((tm, D_MODEL), lambda i, l: (i, 0)),
            scratch_shapes=[pltpu.VMEM((tm, A_WIDTH), F32), pltpu.VMEM((tm, D_MODEL), BF16)],
        ),
        out_shape=jax.ShapeDtypeStruct((m, D_MODEL), F32),
        compiler_params=_params(("arbitrary",)),
        name="mix_out",
    )(lid, proj, proj, proj, proj, proj, proj, proj, proj, proj, proj, proj, proj, o, h, p,
      gsgu, ws, bs, cw, cb, gout, wout, gple, wgate, wpp)


SRC_AB = 0
SRC_Q = 3 * A_WIDTH + 4 * B_WIDTH
SRC_CKV = SRC_Q + C_HEADS * (C_NOPE + C_ROPE)
SRC_KR = SRC_CKV + KV_RANK
SRC_CZ = SRC_KR + C_ROPE
IN_WIDTH = SRC_CZ + C_WIDTH


def _w_in_kernel(w_ref, o_ref):
    def put(dst, src, width):
        o_ref[:, dst:dst + width] = w_ref[:, src:src + width].astype(BF16)

    for hd in range(C_HEADS):
        src = SRC_Q + hd * (C_NOPE + C_ROPE)
        put(COL_QN + hd * C_NOPE, src, C_NOPE)
        put(COL_QR + hd * C_ROPE, src + C_NOPE, C_ROPE)
    put(COL_CZ, SRC_CZ, C_WIDTH)
    put(COL_CKV, SRC_CKV, KV_RANK)
    put(COL_AU, SRC_AB, SRC_Q)
    put(COL_KR, SRC_KR, C_ROPE)
    put(COL_KR + C_ROPE, SRC_KR, C_ROPE)
    o_ref[:, IN_USED:IN_PAD] = jnp.zeros((o_ref.shape[0], IN_PAD - IN_USED), BF16)


def _permute_w_in(w):
    depth = w.shape[0]
    return pl.pallas_call(
        _w_in_kernel,
        grid=(depth, D_MODEL // TM_WPREP),
        in_specs=[pl.BlockSpec((None, TM_WPREP, IN_WIDTH), lambda l, i: (l, i, 0))],
        out_specs=pl.BlockSpec((None, TM_WPREP, IN_PAD), lambda l, i: (l, i, 0)),
        out_shape=jax.ShapeDtypeStruct((depth, D_MODEL, IN_PAD), BF16),
        compiler_params=_params(("arbitrary", "arbitrary")),
        name="w_in_prep",
    )(w)


def _permute_w_ukv(w):
    depth = w.shape[0]
    kv = w.astype(BF16).reshape(depth, KV_RANK, C_HEADS, C_NOPE + C_V)
    kn = kv[..., :C_NOPE].reshape(depth, KV_RANK, C_HEADS * C_NOPE)
    v = kv[..., C_NOPE:].reshape(depth, KV_RANK, C_HEADS * C_V)
    return jnp.concatenate([kn, v], axis=2)


def _score_shift(gqn, gqr, gkn, gkr):
    amax = lambda g: jnp.max(jnp.abs(g), axis=-1)
    bound = Q_SCALE * (C_NOPE * amax(gqn) * amax(gkn) + C_ROPE * amax(gqr) * amax(gkr))
    shift = jnp.ceil(bound * 1.02)
    bounded = shift <= MAX_SHIFT
    return jnp.where(bounded, shift, 0.0), bounded


def kernel(x, p, positions, attn_norm, w_in, sgu_norm, w_spatial, b_spatial, conv_w, conv_b, kv_norm, w_ukv,
           q_nope_norm, q_rope_norm, k_nope_norm, k_rope_norm, out_norm, w_out, ple_norm, w_ple_gate, w_ple_proj):
    bsz, s_len, d = x.shape
    depth = w_in.shape[0]
    m = bsz * s_len

    inv = 1.0 / (ROPE_BASE ** (jnp.arange(0, C_ROPE, 2, dtype=F32) / C_ROPE))
    ang = positions.astype(F32)[..., None] * inv
    cos, sin = jnp.cos(ang), jnp.sin(ang)
    cos128 = jnp.tile(cos, (1, 1, 4)).reshape(m, 128)
    sin128 = jnp.tile(jnp.concatenate([-sin, sin], axis=-1), (1, 1, 2)).reshape(m, 128)

    vec = lambda g: g[:, None, :]
    w_in_p = _permute_w_in(w_in)
    w_ukv_p = _permute_w_ukv(w_ukv)
    shift, bounded = _score_shift(q_nope_norm, q_rope_norm, k_nope_norm, k_rope_norm)
    kpad = jnp.zeros((depth, 1, 128), F32).at[:, 0, C_ROPE].set(-shift)
    gqr2 = vec(jnp.tile(q_rope_norm, (1, 2)))
    gkr2 = vec(jnp.tile(k_rope_norm, (1, 2)))
    bs_full = jnp.broadcast_to(b_spatial[..., None], (depth, A_HEADS, CHUNK, A_HEAD))
    ws_b, wout_b = w_spatial.astype(BF16), w_out.astype(BF16)
    wgate_b, wpp_b = w_ple_gate.astype(BF16), w_ple_proj.astype(BF16)
    p2 = p.reshape(depth, m, PLE_DIM)

    h = x.reshape(m, d)
    for i in range(depth):
        lid = jnp.full((1,), i, jnp.int32)
        proj = _in_proj(lid, h, vec(attn_norm), w_in_p)
        q, k, v = _attn_prep(lid, proj, cos128, sin128, vec(q_nope_norm), gqr2, vec(kv_norm), w_ukv_p,
                             vec(k_nope_norm), gkr2, kpad, bsz, s_len)
        o = _flash(q, k, v, bounded[i])
        h = _mix_out(lid, proj, o, h, p2, sgu_norm, ws_b, bs_full, conv_w, vec(conv_b), vec(out_norm),
                     wout_b, vec(ple_norm), wgate_b, wpp_b, s_len)
    return h.reshape(bsz, s_len, d)
```

```python
import functools
import math

import jax
import jax.numpy as jnp
from jax import lax
from jax.experimental import pallas as pl
from jax.experimental.pallas import tpu as pltpu

F32 = jnp.float32
BF16 = jnp.bfloat16

D_MODEL = 2048
PLE_DIM = 256
EPS = 1e-6
CHUNK = 128
A_HEAD = 128
A_WIDTH = 512
A_HEADS = 4
B_WIDTH = 512
CONV_W = 3
C_WIDTH = 1024
C_V = 128
C_HEADS = 8
C_NOPE = 128
C_ROPE = 64
KV_RANK = 512
ROPE_BASE = 10000.0
QK_PAD = 256
V_PAD = 256
MAX_SHIFT = 50.0

COL_AU, COL_AV, COL_AZ = 0, 512, 1024
COL_BB, COL_BC, COL_BH, COL_BZ = 1536, 2048, 2560, 3072
COL_CZ = 3584
MIX_COLS = 4608
COL_QN = 4608
COL_QR = 5632
COL_CKV = 6144
COL_KR = 6656
IN_USED = 6784
IN_PAD = 6912
PREP_COLS = IN_PAD - MIX_COLS

Q_SCALE = (C_NOPE + C_ROPE) ** -0.5 * math.log2(math.e)

VMEM_LIMIT = 56 * 1024 * 1024

TM_IN, TN_IN = 512, 768
VMEM_LIMIT_IN = 60 * 1024 * 1024
TM_PREP = 512
TM_WPREP = 256
TQ, TK = 1024, 256
TM_MIX = 256
HALO = 16


def _rms(x, gain):
    ms = jnp.mean(x * x, axis=-1, keepdims=True)
    return x * lax.rsqrt(ms + EPS) * gain


def _silu(z):
    return z * jax.nn.sigmoid(z)


def _layer_spec(shape):
    return pl.BlockSpec((None,) + shape, lambda *a: (a[-1][0],) + (0,) * len(shape))


def _params(semantics):
    return pltpu.CompilerParams(dimension_semantics=semantics, vmem_limit_bytes=VMEM_LIMIT)


def _in_proj_kernel(lid_ref, x_ref, g_ref, w_ref, o_ref, hn_ref):
    hn_ref[...] = _rms(x_ref[...], g_ref[...]).astype(BF16)
    for j in range(IN_PAD // TN_IN):
        cols = slice(j * TN_IN, (j + 1) * TN_IN)
        o_ref[:, cols] = jnp.dot(hn_ref[...], w_ref[:, cols], preferred_element_type=F32).astype(o_ref.dtype)


def _in_proj(lid, h, gain, w):
    m = h.shape[0]
    return pl.pallas_call(
        _in_proj_kernel,
        grid_spec=pltpu.PrefetchScalarGridSpec(
            num_scalar_prefetch=1,
            grid=(m // TM_IN,),
            in_specs=[
                pl.BlockSpec((TM_IN, D_MODEL), lambda i, l: (i, 0)),
                _layer_spec((1, D_MODEL)),
                pl.BlockSpec((None, D_MODEL, IN_PAD), lambda i, l: (l[0], 0, 0), pipeline_mode=pl.Buffered(1)),
            ],
            out_specs=pl.BlockSpec((TM_IN, IN_PAD), lambda i, l: (i, 0)),
            scratch_shapes=[pltpu.VMEM((TM_IN, D_MODEL), BF16)],
        ),
        out_shape=jax.ShapeDtypeStruct((m, IN_PAD), BF16),
        compiler_params=pltpu.CompilerParams(dimension_semantics=("arbitrary",), vmem_limit_bytes=VMEM_LIMIT_IN),
        name="in_proj",
    )(lid, h, gain, w)


def _prep_kernel(lid_ref, x_ref, cos_ref, sin_ref, gqn_ref, gqr_ref, gkv_ref,
                 wukv_ref, gkn_ref, gkr_ref, kpad_ref, q_out, k_out, v_out):
    tm = x_ref.shape[0]
    piece = lambda col, width: x_ref[:, col - MIX_COLS:col - MIX_COLS + width].astype(F32)
    cos = cos_ref[...]
    sin = sin_ref[...]
    lane = lax.broadcasted_iota(jnp.int32, (tm, 128), 1)
    low_half = (lane & 32) == 0
    first = lane < 64

    def rope(x):
        swapped = jnp.where(low_half, pltpu.roll(x, 96, 1), pltpu.roll(x, 32, 1))
        return x * cos + swapped * sin

    for hd in range(C_HEADS):
        x = piece(COL_QN + hd * C_NOPE, C_NOPE)
        q_out[hd, :, 0:C_NOPE] = (_rms(x, gqn_ref[...]) * Q_SCALE).astype(BF16)

    q_pad = jnp.where(lane == C_ROPE, 1.0, 0.0)
    for j in range(C_HEADS // 2):
        x = piece(COL_QR + j * 128, 128)
        x2 = x * x
        s_lo = jnp.sum(jnp.where(first, x2, 0.0), axis=-1, keepdims=True)
        s_hi = jnp.sum(jnp.where(first, 0.0, x2), axis=-1, keepdims=True)
        ms = jnp.where(first, s_lo, s_hi) * (1.0 / C_ROPE)
        r = rope(x * lax.rsqrt(ms + EPS) * gqr_ref[...]) * Q_SCALE
        q_out[2 * j, :, C_NOPE:QK_PAD] = jnp.where(first, r, q_pad).astype(BF16)
        q_out[2 * j + 1, :, C_NOPE:QK_PAD] = jnp.where(first, pltpu.roll(r, 64, 1), q_pad).astype(BF16)

    cn = _rms(piece(COL_CKV, KV_RANK), gkv_ref[...]).astype(BF16)
    kv = jnp.dot(cn, wukv_ref[...], preferred_element_type=F32)
    kr = rope(_rms(piece(COL_KR, 128), gkr_ref[...]))
    kr = jnp.where(first, kr, kpad_ref[...]).astype(BF16)
    ones_col = jnp.where(lane == 0, 1.0, 0.0).astype(BF16)
    for hd in range(C_HEADS):
        kn = kv[:, hd * C_NOPE:(hd + 1) * C_NOPE]
        k_out[hd, :, 0:C_NOPE] = _rms(kn, gkn_ref[...]).astype(BF16)
        k_out[hd, :, C_NOPE:QK_PAD] = kr
        v_out[hd, :, 0:C_V] = kv[:, C_WIDTH + hd * C_V:C_WIDTH + (hd + 1) * C_V].astype(BF16)
        v_out[hd, :, C_V:V_PAD] = ones_col


def _attn_prep(lid, proj, cos128, sin128, gqn, gqr, gkv, wukv, gkn, gkr, kpad, bsz, s_len):
    tm = TM_PREP
    nst = s_len // tm
    head_out = lambda width: pl.BlockSpec((None, C_HEADS, tm, width), lambda i, l: (i // nst, 0, i % nst, 0))
    head_shape = lambda width: jax.ShapeDtypeStruct((bsz, C_HEADS, s_len, width), BF16)
    return pl.pallas_call(
        _prep_kernel,
        grid_spec=pltpu.PrefetchScalarGridSpec(
            num_scalar_prefetch=1,
            grid=(bsz * nst,),
            in_specs=[
                pl.BlockSpec((tm, PREP_COLS), lambda i, l: (i, MIX_COLS // PREP_COLS)),
                pl.BlockSpec((tm, 128), lambda i, l: (i, 0)), pl.BlockSpec((tm, 128), lambda i, l: (i, 0)),
                _layer_spec((1, C_NOPE)), _layer_spec((1, 128)), _layer_spec((1, KV_RANK)),
                _layer_spec((KV_RANK, 2 * C_WIDTH)), _layer_spec((1, C_NOPE)), _layer_spec((1, 128)),
                _layer_spec((1, 128)),
            ],
            out_specs=[head_out(QK_PAD), head_out(QK_PAD), head_out(V_PAD)],
        ),
        out_shape=[head_shape(QK_PAD), head_shape(QK_PAD), head_shape(V_PAD)],
        compiler_params=_params(("arbitrary",)),
        name="attn_prep",
    )(lid, proj, cos128, sin128, gqn, gqr, gkv, wukv, gkn, gkr, kpad)


_NT = (((1,), (1,)), ((), ()))


def _flash_bounded_kernel(q_ref, k_ref, v_ref, o_ref):
    tq = q_ref.shape[0]
    s_len = k_ref.shape[0]
    q = q_ref[...]
    nk = s_len // TK

    def probs(j):
        s = lax.dot_general(q, k_ref[j * TK:(j + 1) * TK, :], _NT, preferred_element_type=F32)
        return jnp.exp2(s).astype(BF16)

    p = probs(0)
    acc = jnp.zeros((tq, V_PAD), F32)
    for j in range(nk):
        p_next = probs(j + 1) if j + 1 < nk else None
        acc = acc + jnp.dot(p, v_ref[j * TK:(j + 1) * TK, :], preferred_element_type=F32)
        p = p_next
    o_ref[...] = (acc[:, 0:C_V] / acc[:, C_V:C_V + 1]).astype(o_ref.dtype)


def _flash_online_kernel(q_ref, k_ref, v_ref, o_ref):
    tq = q_ref.shape[0]
    s_len = k_ref.shape[0]
    q = q_ref[...]

    def body(j, carry):
        m, acc = carry
        start = pl.multiple_of(j * TK, TK)
        s = lax.dot_general(q, k_ref[pl.ds(start, TK), :], _NT, preferred_element_type=F32)
        m_new = jnp.maximum(m, jnp.max(s, axis=-1, keepdims=True))
        p = jnp.exp2(s - m_new).astype(BF16)
        acc = jnp.exp2(m - m_new) * acc + jnp.dot(p, v_ref[pl.ds(start, TK), :], preferred_element_type=F32)
        return m_new, acc

    init = (jnp.full((tq, 1), -jnp.inf, F32), jnp.zeros((tq, V_PAD), F32))
    _, acc = lax.fori_loop(0, s_len // TK, body, init)
    o_ref[...] = (acc[:, 0:C_V] / acc[:, C_V:C_V + 1]).astype(o_ref.dtype)


def _flash(q, k, v, bounded):
    bsz, heads, s_len, _ = q.shape
    nq = s_len // TQ

    def call(body, name):
        return pl.pallas_call(
            body,
            grid=(bsz, heads, nq),
            in_specs=[
                pl.BlockSpec((None, None, TQ, QK_PAD), lambda b, hd, i: (b, hd, i, 0)),
                pl.BlockSpec((None, None, s_len, QK_PAD), lambda b, hd, i: (b, hd, 0, 0)),
                pl.BlockSpec((None, None, s_len, V_PAD), lambda b, hd, i: (b, hd, 0, 0)),
            ],
            out_specs=pl.BlockSpec((TQ, C_V), lambda b, hd, i: (b * nq + i, hd)),
            out_shape=jax.ShapeDtypeStruct((bsz * s_len, C_WIDTH), BF16),
            compiler_params=_params(("arbitrary", "arbitrary", "arbitrary")),
            name=name,
        )(q, k, v)

    return lax.cond(bounded,
                    lambda: call(_flash_bounded_kernel, "flash_bounded"),
                    lambda: call(_flash_online_kernel, "flash_online"))


def _mix_kernel(lid_ref, x_ref, xp_ref, xn_ref, o_ref, h_ref, p_ref,
                gsgu_ref, ws_ref, bs_ref, cw_ref, cb_ref, gout_ref, wout_ref, gple_ref, wgate_ref, wpp_ref,
                out_ref, ya_ref, y_ref, *, s_len):
    tm = h_ref.shape[0]
    i = pl.program_id(0)
    piece = lambda col, width: x_ref[:, col:col + width].astype(F32)

    for c in range(tm // CHUNK):
        rows = slice(c * CHUNK, (c + 1) * CHUNK)
        for hd in range(A_HEADS):
            sub = lambda col: x_ref[rows, col + hd * A_HEAD:col + (hd + 1) * A_HEAD].astype(F32)
            vn = _rms(sub(COL_AV), gsgu_ref[hd:hd + 1, :]).astype(BF16)
            s = jnp.dot(ws_ref[hd], vn, preferred_element_type=F32) + bs_ref[hd]
            ya_ref[rows, hd * A_HEAD:(hd + 1) * A_HEAD] = sub(COL_AU) * s * _silu(sub(COL_AZ))
    y_ref[:, 0:A_WIDTH] = _rms(ya_ref[...], gout_ref[:, 0:A_WIDTH]).astype(BF16)

    xc = piece(COL_BC, B_WIDTH) * piece(COL_BH, B_WIDTH)
    row = lax.broadcasted_iota(jnp.int32, xc.shape, 0)
    has_prev = ((i * tm) % s_len != 0).astype(F32)
    has_next = (((i + 1) * tm) % s_len != 0).astype(F32)
    halo = lambda ref, r: ref[r:r + 1, 0:B_WIDTH].astype(F32) * ref[r:r + 1, B_WIDTH:2 * B_WIDTH].astype(F32)
    prev_row = halo(xp_ref, HALO - 1) * has_prev
    next_row = halo(xn_ref, 0) * has_next
    x_prev = jnp.where(row == 0, prev_row, pltpu.roll(xc, 1, 0))
    x_next = jnp.where(row == tm - 1, next_row, pltpu.roll(xc, tm - 1, 0))
    conv = cb_ref[...] + x_prev * cw_ref[0:1, :] + xc * cw_ref[1:2, :] + x_next * cw_ref[2:3, :]
    yb = piece(COL_BB, B_WIDTH) * conv * _silu(piece(COL_BZ, B_WIDTH))
    y_ref[:, A_WIDTH:A_WIDTH + B_WIDTH] = _rms(yb, gout_ref[:, A_WIDTH:A_WIDTH + B_WIDTH]).astype(BF16)

    yc = o_ref[...].astype(F32) * _silu(piece(COL_CZ, C_WIDTH))
    y_ref[:, A_WIDTH + B_WIDTH:] = _rms(yc, gout_ref[:, A_WIDTH + B_WIDTH:]).astype(BF16)

    hm = h_ref[...] + jnp.dot(y_ref[...], wout_ref[...], preferred_element_type=F32)
    hn = _rms(hm, gple_ref[...]).astype(BF16)
    gate = jax.nn.sigmoid(jnp.dot(hn, wgate_ref[...], preferred_element_type=F32))
    pp = jnp.dot(p_ref[...].astype(BF16), wpp_ref[...], preferred_element_type=F32)
    out_ref[...] = hm + gate * pp


def _mix_out(lid, proj, o, h, p, gsgu, ws, bs, cw, cb, gout, wout, gple, wgate, wpp, s_len):
    m = h.shape[0]
    tm = TM_MIX
    nhalo = tm // HALO
    last_halo = m // HALO - 1
    halo_col = COL_BC // (2 * B_WIDTH)
    return pl.pallas_call(
        functools.partial(_mix_kernel, s_len=s_len),
        grid_spec=pltpu.PrefetchScalarGridSpec(
            num_scalar_prefetch=1,
            grid=(m // tm,),
            in_specs=[
                pl.BlockSpec((tm, MIX_COLS), lambda i, l: (i, 0)),
                pl.BlockSpec((HALO, 2 * B_WIDTH), lambda i, l: (jnp.maximum(i * nhalo - 1, 0), halo_col)),
                pl.BlockSpec((HALO, 2 * B_WIDTH), lambda i, l: (jnp.minimum((i + 1) * nhalo, last_halo), halo_col)),
                pl.BlockSpec((tm, C_WIDTH), lambda i, l: (i, 0)),
                pl.BlockSpec((tm, D_MODEL), lambda i, l: (i, 0)),
                pl.BlockSpec((None, tm, PLE_DIM), lambda i, l: (l[0], i, 0)),
                _layer_spec((A_HEADS, A_HEAD)), _layer_spec((A_HEADS, CHUNK, CHUNK)),
                _layer_spec((A_HEADS, CHUNK, A_HEAD)),
                _layer_spec((CONV_W, B_WIDTH)), _layer_spec((1, B_WIDTH)), _layer_spec((1, D_MODEL)),
                _layer_spec((D_MODEL, D_MODEL)), _layer_spec((1, D_MODEL)), _layer_spec((D_MODEL, D_MODEL)),
                _layer_spec((PLE_DIM, D_MODEL)),
            ],
            out_specs=pl.BlockSpec((tm, D_MODEL), lambda i, l: (i, 0)),
            scratch_shapes=[pltpu.VMEM((tm, A_WIDTH), F32), pltpu.VMEM((tm, D_MODEL), BF16)],
        ),
        out_shape=jax.ShapeDtypeStruct((m, D_MODEL), F32),
        compiler_params=_params(("arbitrary",)),
        name="mix_out",
    )(lid, proj, proj, proj, o, h, p, gsgu, ws, bs, cw, cb, gout, wout, gple, wgate, wpp)


SRC_AB = 0
SRC_Q = 3 * A_WIDTH + 4 * B_WIDTH
SRC_CKV = SRC_Q + C_HEADS * (C_NOPE + C_ROPE)
SRC_KR = SRC_CKV + KV_RANK
SRC_CZ = SRC_KR + C_ROPE
IN_WIDTH = SRC_CZ + C_WIDTH


def _w_in_kernel(wt_ref, o_ref):
    def put(dst, *pieces):
        rows = [wt_ref[src:src + n, :] for src, n in pieces]
        block = rows[0] if len(rows) == 1 else jnp.concatenate(rows, axis=0)
        o_ref[:, dst:dst + 128] = block.T.astype(BF16)

    for hd in range(C_HEADS):
        put(COL_QN + hd * C_NOPE, (SRC_Q + hd * (C_NOPE + C_ROPE), C_NOPE))
    for j in range(C_HEADS // 2):
        rope = lambda hd: (SRC_Q + hd * (C_NOPE + C_ROPE) + C_NOPE, C_ROPE)
        put(COL_QR + j * 128, rope(2 * j), rope(2 * j + 1))
    for dst, src, width in ((COL_CZ, SRC_CZ, C_WIDTH), (COL_CKV, SRC_CKV, KV_RANK), (COL_AU, SRC_AB, SRC_Q)):
        for c in range(width // 128):
            put(dst + c * 128, (src + c * 128, 128))
    put(COL_KR, (SRC_KR, C_ROPE), (SRC_KR, C_ROPE))
    o_ref[:, IN_USED:IN_PAD] = jnp.zeros((o_ref.shape[0], IN_PAD - IN_USED), BF16)


def _permute_w_in(w):
    depth = w.shape[0]
    return pl.pallas_call(
        _w_in_kernel,
        grid=(depth, D_MODEL // TM_WPREP),
        in_specs=[pl.BlockSpec((None, IN_WIDTH, TM_WPREP), lambda l, i: (l, 0, i))],
        out_specs=pl.BlockSpec((None, TM_WPREP, IN_PAD), lambda l, i: (l, i, 0)),
        out_shape=jax.ShapeDtypeStruct((depth, D_MODEL, IN_PAD), BF16),
        compiler_params=_params(("arbitrary", "arbitrary")),
        name="w_in_prep",
    )(jnp.swapaxes(w, 1, 2))


def _permute_w_ukv(w):
    depth = w.shape[0]
    kv = w.astype(BF16).reshape(depth, KV_RANK, C_HEADS, C_NOPE + C_V)
    kn = kv[..., :C_NOPE].reshape(depth, KV_RANK, C_HEADS * C_NOPE)
    v = kv[..., C_NOPE:].reshape(depth, KV_RANK, C_HEADS * C_V)
    return jnp.concatenate([kn, v], axis=2)


def _score_shift(gqn, gqr, gkn, gkr):
    amax = lambda g: jnp.max(jnp.abs(g), axis=-1)
    bound = Q_SCALE * (C_NOPE * amax(gqn) * amax(gkn) + C_ROPE * amax(gqr) * amax(gkr))
    shift = jnp.ceil(bound * 1.02)
    bounded = shift <= MAX_SHIFT
    return jnp.where(bounded, shift, 0.0), bounded


def kernel(x, p, positions, attn_norm, w_in, sgu_norm, w_spatial, b_spatial, conv_w, conv_b, kv_norm, w_ukv,
           q_nope_norm, q_rope_norm, k_nope_norm, k_rope_norm, out_norm, w_out, ple_norm, w_ple_gate, w_ple_proj):
    bsz, s_len, d = x.shape
    depth = w_in.shape[0]
    m = bsz * s_len

    inv = 1.0 / (ROPE_BASE ** (jnp.arange(0, C_ROPE, 2, dtype=F32) / C_ROPE))
    ang = positions.astype(F32)[..., None] * inv
    cos, sin = jnp.cos(ang), jnp.sin(ang)
    cos128 = jnp.tile(cos, (1, 1, 4)).reshape(m, 128)
    sin128 = jnp.tile(jnp.concatenate([-sin, sin], axis=-1), (1, 1, 2)).reshape(m, 128)

    vec = lambda g: g[:, None, :]
    w_in_p = _permute_w_in(w_in)
    w_ukv_p = _permute_w_ukv(w_ukv)
    shift, bounded = _score_shift(q_nope_norm, q_rope_norm, k_nope_norm, k_rope_norm)
    kpad = jnp.zeros((depth, 1, 128), F32).at[:, 0, C_ROPE].set(-shift)
    gqr2 = vec(jnp.tile(q_rope_norm, (1, 2)))
    gkr2 = vec(jnp.tile(k_rope_norm, (1, 2)))
    bs_full = jnp.broadcast_to(b_spatial[..., None], (depth, A_HEADS, CHUNK, A_HEAD))
    ws_b, wout_b = w_spatial.astype(BF16), w_out.astype(BF16)
    wgate_b, wpp_b = w_ple_gate.astype(BF16), w_ple_proj.astype(BF16)
    p2 = p.reshape(depth, m, PLE_DIM)

    h = x.reshape(m, d)
    for i in range(depth):
        lid = jnp.full((1,), i, jnp.int32)
        proj = _in_proj(lid, h, vec(attn_norm), w_in_p)
        q, k, v = _attn_prep(lid, proj, cos128, sin128, vec(q_nope_norm), gqr2, vec(kv_norm), w_ukv_p,
                             vec(k_nope_norm), gkr2, kpad, bsz, s_len)
        o = _flash(q, k, v, bounded[i])
        h = _mix_out(lid, proj, o, h, p2, sgu_norm, ws_b, bs_full, conv_w, vec(conv_b), vec(out_norm),
                     wout_b, vec(ple_norm), wgate_b, wpp_b, s_len)
    return h.reshape(bsz, s_len, d)
```

```python
import functools
import math

import jax
import jax.numpy as jnp
from jax import lax
from jax.experimental import pallas as pl
from jax.experimental.pallas import tpu as pltpu

F32 = jnp.float32
BF16 = jnp.bfloat16

D_MODEL = 2048
PLE_DIM = 256
EPS = 1e-6
CHUNK = 128
A_HEAD = 128
A_WIDTH = 512
A_HEADS = 4
B_WIDTH = 512
CONV_W = 3
C_WIDTH = 1024
C_V = 128
C_HEADS = 8
C_NOPE = 128
C_ROPE = 64
KV_RANK = 512
ROPE_BASE = 10000.0
QK_PAD = 256
V_PAD = 256
MAX_SHIFT = 50.0

COL_AU, COL_AV, COL_AZ = 0, 512, 1024
COL_BB, COL_BC, COL_BH, COL_BZ = 1536, 2048, 2560, 3072
COL_CZ = 3584
MIX_COLS = 4608
COL_QN = 4608
COL_QR = 5632
COL_CKV = 6144
COL_KR = 6656
IN_USED = 6784
IN_PAD = 6912
PREP_COLS = IN_PAD - MIX_COLS

Q_SCALE = (C_NOPE + C_ROPE) ** -0.5 * math.log2(math.e)

VMEM_LIMIT = 56 * 1024 * 1024

TM_IN, TN_IN = 512, 768
VMEM_LIMIT_IN = 60 * 1024 * 1024
TM_PREP = 512
TM_WPREP = 256
TQ, TK = 2048, 256
FLASH_CHAINS = 2
TM_MIX = 256
HALO = 16


def _rms(x, gain):
    ms = jnp.mean(x * x, axis=-1, keepdims=True)
    return x * lax.rsqrt(ms + EPS) * gain


def _silu(z):
    return z * jax.nn.sigmoid(z)


def _layer_spec(shape):
    return pl.BlockSpec((None,) + shape, lambda *a: (a[-1][0],) + (0,) * len(shape))


def _params(semantics):
    return pltpu.CompilerParams(dimension_semantics=semantics, vmem_limit_bytes=VMEM_LIMIT)


def _in_proj_kernel(lid_ref, x_ref, g_ref, w_ref, o_ref, hn_ref):
    hn_ref[...] = _rms(x_ref[...], g_ref[...]).astype(BF16)
    for j in range(IN_PAD // TN_IN):
        cols = slice(j * TN_IN, (j + 1) * TN_IN)
        o_ref[:, cols] = jnp.dot(hn_ref[...], w_ref[:, cols], preferred_element_type=F32).astype(o_ref.dtype)


def _in_proj(lid, h, gain, w):
    m = h.shape[0]
    return pl.pallas_call(
        _in_proj_kernel,
        grid_spec=pltpu.PrefetchScalarGridSpec(
            num_scalar_prefetch=1,
            grid=(m // TM_IN,),
            in_specs=[
                pl.BlockSpec((TM_IN, D_MODEL), lambda i, l: (i, 0)),
                _layer_spec((1, D_MODEL)),
                pl.BlockSpec((None, D_MODEL, IN_PAD), lambda i, l: (l[0], 0, 0), pipeline_mode=pl.Buffered(1)),
            ],
            out_specs=pl.BlockSpec((TM_IN, IN_PAD), lambda i, l: (i, 0)),
            scratch_shapes=[pltpu.VMEM((TM_IN, D_MODEL), BF16)],
        ),
        out_shape=jax.ShapeDtypeStruct((m, IN_PAD), BF16),
        compiler_params=pltpu.CompilerParams(dimension_semantics=("arbitrary",), vmem_limit_bytes=VMEM_LIMIT_IN),
        name="in_proj",
    )(lid, h, gain, w)


def _prep_kernel(lid_ref, x_ref, cos_ref, sin_ref, gqn_ref, gqr_ref, gkv_ref,
                 wukv_ref, gkn_ref, gkr_ref, kpad_ref, q_out, k_out, v_out):
    tm = x_ref.shape[0]
    piece = lambda col, width: x_ref[:, col - MIX_COLS:col - MIX_COLS + width].astype(F32)
    cos = cos_ref[...]
    sin = sin_ref[...]
    lane = lax.broadcasted_iota(jnp.int32, (tm, 128), 1)
    low_half = (lane & 32) == 0
    first = lane < 64

    def rope(x):
        swapped = jnp.where(low_half, pltpu.roll(x, 96, 1), pltpu.roll(x, 32, 1))
        return x * cos + swapped * sin

    for hd in range(C_HEADS):
        x = piece(COL_QN + hd * C_NOPE, C_NOPE)
        q_out[hd, :, 0:C_NOPE] = (_rms(x, gqn_ref[...]) * Q_SCALE).astype(BF16)

    q_pad = jnp.where(lane == C_ROPE, 1.0, 0.0)
    for j in range(C_HEADS // 2):
        x = piece(COL_QR + j * 128, 128)
        x2 = x * x
        s_lo = jnp.sum(jnp.where(first, x2, 0.0), axis=-1, keepdims=True)
        s_hi = jnp.sum(jnp.where(first, 0.0, x2), axis=-1, keepdims=True)
        ms = jnp.where(first, s_lo, s_hi) * (1.0 / C_ROPE)
        r = rope(x * lax.rsqrt(ms + EPS) * gqr_ref[...]) * Q_SCALE
        q_out[2 * j, :, C_NOPE:QK_PAD] = jnp.where(first, r, q_pad).astype(BF16)
        q_out[2 * j + 1, :, C_NOPE:QK_PAD] = jnp.where(first, pltpu.roll(r, 64, 1), q_pad).astype(BF16)

    cn = _rms(piece(COL_CKV, KV_RANK), gkv_ref[...]).astype(BF16)
    kv = jnp.dot(cn, wukv_ref[...], preferred_element_type=F32)
    kr = rope(_rms(piece(COL_KR, 128), gkr_ref[...]))
    kr = jnp.where(first, kr, kpad_ref[...]).astype(BF16)
    ones_col = jnp.where(lane == 0, 1.0, 0.0).astype(BF16)
    for hd in range(C_HEADS):
        kn = kv[:, hd * C_NOPE:(hd + 1) * C_NOPE]
        k_out[hd, :, 0:C_NOPE] = _rms(kn, gkn_ref[...]).astype(BF16)
        k_out[hd, :, C_NOPE:QK_PAD] = kr
        v_out[hd, :, 0:C_V] = kv[:, C_WIDTH + hd * C_V:C_WIDTH + (hd + 1) * C_V].astype(BF16)
        v_out[hd, :, C_V:V_PAD] = ones_col


def _attn_prep(lid, proj, cos128, sin128, gqn, gqr, gkv, wukv, gkn, gkr, kpad, bsz, s_len):
    tm = TM_PREP
    nst = s_len // tm
    head_out = lambda width: pl.BlockSpec((None, C_HEADS, tm, width), lambda i, l: (i // nst, 0, i % nst, 0))
    head_shape = lambda width: jax.ShapeDtypeStruct((bsz, C_HEADS, s_len, width), BF16)
    return pl.pallas_call(
        _prep_kernel,
        grid_spec=pltpu.PrefetchScalarGridSpec(
            num_scalar_prefetch=1,
            grid=(bsz * nst,),
            in_specs=[
                pl.BlockSpec((tm, PREP_COLS), lambda i, l: (i, MIX_COLS // PREP_COLS)),
                pl.BlockSpec((tm, 128), lambda i, l: (i, 0)), pl.BlockSpec((tm, 128), lambda i, l: (i, 0)),
                _layer_spec((1, C_NOPE)), _layer_spec((1, 128)), _layer_spec((1, KV_RANK)),
                _layer_spec((KV_RANK, 2 * C_WIDTH)), _layer_spec((1, C_NOPE)), _layer_spec((1, 128)),
                _layer_spec((1, 128)),
            ],
            out_specs=[head_out(QK_PAD), head_out(QK_PAD), head_out(V_PAD)],
        ),
        out_shape=[head_shape(QK_PAD), head_shape(QK_PAD), head_shape(V_PAD)],
        compiler_params=_params(("arbitrary",)),
        name="attn_prep",
    )(lid, proj, cos128, sin128, gqn, gqr, gkv, wukv, gkn, gkr, kpad)


_NT = (((1,), (1,)), ((), ()))


def _flash_bounded_kernel(q_ref, k_ref, v_ref, o_ref):
    rows = q_ref.shape[0] // FLASH_CHAINS
    s_len = k_ref.shape[0]
    nk = s_len // TK

    def probs(c, j):
        q = q_ref[c * rows:(c + 1) * rows, :]
        s = lax.dot_general(q, k_ref[j * TK:(j + 1) * TK, :], _NT, preferred_element_type=F32)
        return jnp.exp2(s).astype(BF16)

    p = [None] * FLASH_CHAINS
    acc = [jnp.zeros((rows, V_PAD), F32)] * FLASH_CHAINS
    for t in range(-1, nk + FLASH_CHAINS - 1):
        for c in range(FLASH_CHAINS):
            j = t - c
            p_next = probs(c, j + 1) if 0 <= j + 1 < nk else None
            if 0 <= j < nk:
                acc[c] = acc[c] + jnp.dot(p[c], v_ref[j * TK:(j + 1) * TK, :], preferred_element_type=F32)
            if p_next is not None:
                p[c] = p_next
    for c in range(FLASH_CHAINS):
        o_ref[c * rows:(c + 1) * rows, :] = (acc[c][:, 0:C_V] / acc[c][:, C_V:C_V + 1]).astype(o_ref.dtype)


def _flash_online_kernel(q_ref, k_ref, v_ref, o_ref):
    tq = q_ref.shape[0]
    s_len = k_ref.shape[0]
    q = q_ref[...]

    def body(j, carry):
        m, acc = carry
        start = pl.multiple_of(j * TK, TK)
        s = lax.dot_general(q, k_ref[pl.ds(start, TK), :], _NT, preferred_element_type=F32)
        m_new = jnp.maximum(m, jnp.max(s, axis=-1, keepdims=True))
        p = jnp.exp2(s - m_new).astype(BF16)
        acc = jnp.exp2(m - m_new) * acc + jnp.dot(p, v_ref[pl.ds(start, TK), :], preferred_element_type=F32)
        return m_new, acc

    init = (jnp.full((tq, 1), -jnp.inf, F32), jnp.zeros((tq, V_PAD), F32))
    _, acc = lax.fori_loop(0, s_len // TK, body, init)
    o_ref[...] = (acc[:, 0:C_V] / acc[:, C_V:C_V + 1]).astype(o_ref.dtype)


def _flash(q, k, v, bounded):
    bsz, heads, s_len, _ = q.shape
    nq = s_len // TQ

    def call(body, name):
        return pl.pallas_call(
            body,
            grid=(bsz, heads, nq),
            in_specs=[
                pl.BlockSpec((None, None, TQ, QK_PAD), lambda b, hd, i: (b, hd, i, 0)),
                pl.BlockSpec((None, None, s_len, QK_PAD), lambda b, hd, i: (b, hd, 0, 0)),
                pl.BlockSpec((None, None, s_len, V_PAD), lambda b, hd, i: (b, hd, 0, 0)),
            ],
            out_specs=pl.BlockSpec((TQ, C_V), lambda b, hd, i: (b * nq + i, hd)),
            out_shape=jax.ShapeDtypeStruct((bsz * s_len, C_WIDTH), BF16),
            compiler_params=_params(("arbitrary", "arbitrary", "arbitrary")),
            name=name,
        )(q, k, v)

    return lax.cond(bounded,
                    lambda: call(_flash_bounded_kernel, "flash_bounded"),
                    lambda: call(_flash_online_kernel, "flash_online"))


def _mix_kernel(lid_ref, x_ref, xp_ref, xn_ref, o_ref, h_ref, p_ref,
                gsgu_ref, ws_ref, bs_ref, cw_ref, cb_ref, gout_ref, wout_ref, gple_ref, wgate_ref, wpp_ref,
                out_ref, ya_ref, y_ref, *, s_len):
    tm = h_ref.shape[0]
    i = pl.program_id(0)
    piece = lambda col, width: x_ref[:, col:col + width].astype(F32)

    for c in range(tm // CHUNK):
        rows = slice(c * CHUNK, (c + 1) * CHUNK)
        for hd in range(A_HEADS):
            sub = lambda col: x_ref[rows, col + hd * A_HEAD:col + (hd + 1) * A_HEAD].astype(F32)
            vn = _rms(sub(COL_AV), gsgu_ref[hd:hd + 1, :]).astype(BF16)
            s = jnp.dot(ws_ref[hd], vn, preferred_element_type=F32) + bs_ref[hd]
            ya_ref[rows, hd * A_HEAD:(hd + 1) * A_HEAD] = sub(COL_AU) * s * _silu(sub(COL_AZ))
    y_ref[:, 0:A_WIDTH] = _rms(ya_ref[...], gout_ref[:, 0:A_WIDTH]).astype(BF16)

    xc = piece(COL_BC, B_WIDTH) * piece(COL_BH, B_WIDTH)
    row = lax.broadcasted_iota(jnp.int32, xc.shape, 0)
    has_prev = ((i * tm) % s_len != 0).astype(F32)
    has_next = (((i + 1) * tm) % s_len != 0).astype(F32)
    halo = lambda ref, r: ref[r:r + 1, 0:B_WIDTH].astype(F32) * ref[r:r + 1, B_WIDTH:2 * B_WIDTH].astype(F32)
    prev_row = halo(xp_ref, HALO - 1) * has_prev
    next_row = halo(xn_ref, 0) * has_next
    x_prev = jnp.where(row == 0, prev_row, pltpu.roll(xc, 1, 0))
    x_next = jnp.where(row == tm - 1, next_row, pltpu.roll(xc, tm - 1, 0))
    conv = cb_ref[...] + x_prev * cw_ref[0:1, :] + xc * cw_ref[1:2, :] + x_next * cw_ref[2:3, :]
    yb = piece(COL_BB, B_WIDTH) * conv * _silu(piece(COL_BZ, B_WIDTH))
    y_ref[:, A_WIDTH:A_WIDTH + B_WIDTH] = _rms(yb, gout_ref[:, A_WIDTH:A_WIDTH + B_WIDTH]).astype(BF16)

    yc = o_ref[...].astype(F32) * _silu(piece(COL_CZ, C_WIDTH))
    y_ref[:, A_WIDTH + B_WIDTH:] = _rms(yc, gout_ref[:, A_WIDTH + B_WIDTH:]).astype(BF16)

    hm = h_ref[...] + jnp.dot(y_ref[...], wout_ref[...], preferred_element_type=F32)
    hn = _rms(hm, gple_ref[...]).astype(BF16)
    gate = jax.nn.sigmoid(jnp.dot(hn, wgate_ref[...], preferred_element_type=F32))
    pp = jnp.dot(p_ref[...].astype(BF16), wpp_ref[...], preferred_element_type=F32)
    out_ref[...] = hm + gate * pp


def _mix_out(lid, proj, o, h, p, gsgu, ws, bs, cw, cb, gout, wout, gple, wgate, wpp, s_len):
    m = h.shape[0]
    tm = TM_MIX
    nhalo = tm // HALO
    last_halo = m // HALO - 1
    halo_col = COL_BC // (2 * B_WIDTH)
    return pl.pallas_call(
        functools.partial(_mix_kernel, s_len=s_len),
        grid_spec=pltpu.PrefetchScalarGridSpec(
            num_scalar_prefetch=1,
            grid=(m // tm,),
            in_specs=[
                pl.BlockSpec((tm, MIX_COLS), lambda i, l: (i, 0)),
                pl.BlockSpec((HALO, 2 * B_WIDTH), lambda i, l: (jnp.maximum(i * nhalo - 1, 0), halo_col)),
                pl.BlockSpec((HALO, 2 * B_WIDTH), lambda i, l: (jnp.minimum((i + 1) * nhalo, last_halo), halo_col)),
                pl.BlockSpec((tm, C_WIDTH), lambda i, l: (i, 0)),
                pl.BlockSpec((tm, D_MODEL), lambda i, l: (i, 0)),
                pl.BlockSpec((None, tm, PLE_DIM), lambda i, l: (l[0], i, 0)),
                _layer_spec((A_HEADS, A_HEAD)), _layer_spec((A_HEADS, CHUNK, CHUNK)),
                _layer_spec((A_HEADS, CHUNK, A_HEAD)),
                _layer_spec((CONV_W, B_WIDTH)), _layer_spec((1, B_WIDTH)), _layer_spec((1, D_MODEL)),
                _layer_spec((D_MODEL, D_MODEL)), _layer_spec((1, D_MODEL)), _layer_spec((D_MODEL, D_MODEL)),
                _layer_spec((PLE_DIM, D_MODEL)),
            ],
            out_specs=pl.BlockSpec((tm, D_MODEL), lambda i, l: (i, 0)),
            scratch_shapes=[pltpu.VMEM((tm, A_WIDTH), F32), pltpu.VMEM((tm, D_MODEL), BF16)],
        ),
        out_shape=jax.ShapeDtypeStruct((m, D_MODEL), F32),
        compiler_params=_params(("arbitrary",)),
        name="mix_out",
    )(lid, proj, proj, proj, o, h, p, gsgu, ws, bs, cw, cb, gout, wout, gple, wgate, wpp)


SRC_AB = 0
SRC_Q = 3 * A_WIDTH + 4 * B_WIDTH
SRC_CKV = SRC_Q + C_HEADS * (C_NOPE + C_ROPE)
SRC_KR = SRC_CKV + KV_RANK
SRC_CZ = SRC_KR + C_ROPE
IN_WIDTH = SRC_CZ + C_WIDTH


def _w_in_kernel(wt_ref, o_ref):
    def put(dst, *pieces):
        rows = [wt_ref[src:src + n, :] for src, n in pieces]
        block = rows[0] if len(rows) == 1 else jnp.concatenate(rows, axis=0)
        o_ref[:, dst:dst + 128] = block.T.astype(BF16)

    for hd in range(C_HEADS):
        put(COL_QN + hd * C_NOPE, (SRC_Q + hd * (C_NOPE + C_ROPE), C_NOPE))
    for j in range(C_HEADS // 2):
        rope = lambda hd: (SRC_Q + hd * (C_NOPE + C_ROPE) + C_NOPE, C_ROPE)
        put(COL_QR + j * 128, rope(2 * j), rope(2 * j + 1))
    for dst, src, width in ((COL_CZ, SRC_CZ, C_WIDTH), (COL_CKV, SRC_CKV, KV_RANK), (COL_AU, SRC_AB, SRC_Q)):
        for c in range(width // 128):
            put(dst + c * 128, (src + c * 128, 128))
    put(COL_KR, (SRC_KR, C_ROPE), (SRC_KR, C_ROPE))
    o_ref[:, IN_USED:IN_PAD] = jnp.zeros((o_ref.shape[0], IN_PAD - IN_USED), BF16)


def _permute_w_in(w):
    depth = w.shape[0]
    return pl.pallas_call(
        _w_in_kernel,
        grid=(depth, D_MODEL // TM_WPREP),
        in_specs=[pl.BlockSpec((None, IN_WIDTH, TM_WPREP), lambda l, i: (l, 0, i))],
        out_specs=pl.BlockSpec((None, TM_WPREP, IN_PAD), lambda l, i: (l, i, 0)),
        out_shape=jax.ShapeDtypeStruct((depth, D_MODEL, IN_PAD), BF16),
        compiler_params=_params(("arbitrary", "arbitrary")),
        name="w_in_prep",
    )(jnp.swapaxes(w, 1, 2))


def _permute_w_ukv(w):
    depth = w.shape[0]
    kv = w.astype(BF16).reshape(depth, KV_RANK, C_HEADS, C_NOPE + C_V)
    kn = kv[..., :C_NOPE].reshape(depth, KV_RANK, C_HEADS * C_NOPE)
    v = kv[..., C_NOPE:].reshape(depth, KV_RANK, C_HEADS * C_V)
    return jnp.concatenate([kn, v], axis=2)


def _score_shift(gqn, gqr, gkn, gkr):
    amax = lambda g: jnp.max(jnp.abs(g), axis=-1)
    bound = Q_SCALE * (C_NOPE * amax(gqn) * amax(gkn) + C_ROPE * amax(gqr) * amax(gkr))
    shift = jnp.ceil(bound * 1.02)
    bounded = shift <= MAX_SHIFT
    return jnp.where(bounded, shift, 0.0), bounded


def kernel(x, p, positions, attn_norm, w_in, sgu_norm, w_spatial, b_spatial, conv_w, conv_b, kv_norm, w_ukv,
           q_nope_norm, q_rope_norm, k_nope_norm, k_rope_norm, out_norm, w_out, ple_norm, w_ple_gate, w_ple_proj):
    bsz, s_len, d = x.shape
    depth = w_in.shape[0]
    m = bsz * s_len

    inv = 1.0 / (ROPE_BASE ** (jnp.arange(0, C_ROPE, 2, dtype=F32) / C_ROPE))
    ang = positions.astype(F32)[..., None] * inv
    cos, sin = jnp.cos(ang), jnp.sin(ang)
    cos128 = jnp.tile(cos, (1, 1, 4)).reshape(m, 128)
    sin128 = jnp.tile(jnp.concatenate([-sin, sin], axis=-1), (1, 1, 2)).reshape(m, 128)

    vec = lambda g: g[:, None, :]
    w_in_p = _permute_w_in(w_in)
    w_ukv_p = _permute_w_ukv(w_ukv)
    shift, bounded = _score_shift(q_nope_norm, q_rope_norm, k_nope_norm, k_rope_norm)
    kpad = jnp.zeros((depth, 1, 128), F32).at[:, 0, C_ROPE].set(-shift)
    gqr2 = vec(jnp.tile(q_rope_norm, (1, 2)))
    gkr2 = vec(jnp.tile(k_rope_norm, (1, 2)))
    bs_full = jnp.broadcast_to(b_spatial[..., None], (depth, A_HEADS, CHUNK, A_HEAD))
    ws_b, wout_b = w_spatial.astype(BF16), w_out.astype(BF16)
    wgate_b, wpp_b = w_ple_gate.astype(BF16), w_ple_proj.astype(BF16)
    p2 = p.reshape(depth, m, PLE_DIM)

    h = x.reshape(m, d)
    for i in range(depth):
        lid = jnp.full((1,), i, jnp.int32)
        proj = _in_proj(lid, h, vec(attn_norm), w_in_p)
        q, k, v = _attn_prep(lid, proj, cos128, sin128, vec(q_nope_norm), gqr2, vec(kv_norm), w_ukv_p,
                             vec(k_nope_norm), gkr2, kpad, bsz, s_len)
        o = _flash(q, k, v, bounded[i])
        h = _mix_out(lid, proj, o, h, p2, sgu_norm, ws_b, bs_full, conv_w, vec(conv_b), vec(out_norm),
                     wout_b, vec(ple_norm), wgate_b, wpp_b, s_len)
    return h.reshape(bsz, s_len, d)
```

```python
import functools
import math

import jax
import jax.numpy as jnp
from jax import lax
from jax.experimental import pallas as pl
from jax.experimental.pallas import tpu as pltpu

F32 = jnp.float32
BF16 = jnp.bfloat16

D_MODEL = 2048
PLE_DIM = 256
EPS = 1e-6
CHUNK = 128
A_HEAD = 128
A_WIDTH = 512
A_HEADS = 4
B_WIDTH = 512
CONV_W = 3
C_WIDTH = 1024
C_V = 128
C_HEADS = 8
C_NOPE = 128
C_ROPE = 64
KV_RANK = 512
ROPE_BASE = 10000.0
QK_PAD = 256
V_PAD = 256
MAX_SHIFT = 50.0

COL_AU, COL_AV, COL_AZ = 0, 512, 1024
COL_BB, COL_BC, COL_BH, COL_BZ = 1536, 2048, 2560, 3072
COL_CZ = 3584
MIX_COLS = 4608
COL_QN = 4608
COL_QR = 5632
COL_CKV = 6144
COL_KR = 6656
IN_USED = 6784
IN_PAD = 6912
PREP_COLS = IN_PAD - MIX_COLS

Q_SCALE = (C_NOPE + C_ROPE) ** -0.5 * math.log2(math.e)

VMEM_LIMIT = 56 * 1024 * 1024

TM_IN, TN_IN = 512, 768
VMEM_LIMIT_IN = 60 * 1024 * 1024
TM_PREP = 512
TM_WPREP = 256
TQ, TK = 2048, 256
FLASH_CHAINS = 2
TM_MIX = 512
HALO = 16


def _rms(x, gain):
    ms = jnp.mean(x * x, axis=-1, keepdims=True)
    return x * lax.rsqrt(ms + EPS) * gain


def _silu(z):
    return z * jax.nn.sigmoid(z)


def _layer_spec(shape, buffers=None):
    mode = {} if buffers is None else {"pipeline_mode": pl.Buffered(buffers)}
    return pl.BlockSpec((None,) + shape, lambda *a: (a[-1][0],) + (0,) * len(shape), **mode)


def _params(semantics):
    return pltpu.CompilerParams(dimension_semantics=semantics, vmem_limit_bytes=VMEM_LIMIT)


def _in_proj_kernel(lid_ref, x_ref, g_ref, w_ref, o_ref, hn_ref):
    hn_ref[...] = _rms(x_ref[...], g_ref[...]).astype(BF16)
    for j in range(IN_PAD // TN_IN):
        cols = slice(j * TN_IN, (j + 1) * TN_IN)
        o_ref[:, cols] = jnp.dot(hn_ref[...], w_ref[:, cols], preferred_element_type=F32).astype(o_ref.dtype)


def _in_proj(lid, h, gain, w):
    m = h.shape[0]
    return pl.pallas_call(
        _in_proj_kernel,
        grid_spec=pltpu.PrefetchScalarGridSpec(
            num_scalar_prefetch=1,
            grid=(m // TM_IN,),
            in_specs=[
                pl.BlockSpec((TM_IN, D_MODEL), lambda i, l: (i, 0)),
                _layer_spec((1, D_MODEL)),
                pl.BlockSpec((None, D_MODEL, IN_PAD), lambda i, l: (l[0], 0, 0), pipeline_mode=pl.Buffered(1)),
            ],
            out_specs=pl.BlockSpec((TM_IN, IN_PAD), lambda i, l: (i, 0)),
            scratch_shapes=[pltpu.VMEM((TM_IN, D_MODEL), BF16)],
        ),
        out_shape=jax.ShapeDtypeStruct((m, IN_PAD), BF16),
        compiler_params=pltpu.CompilerParams(dimension_semantics=("arbitrary",), vmem_limit_bytes=VMEM_LIMIT_IN),
        name="in_proj",
    )(lid, h, gain, w)


def _prep_kernel(lid_ref, x_ref, cos_ref, sin_ref, gqn_ref, gqr_ref, gkv_ref,
                 wukv_ref, gkn_ref, gkr_ref, kpad_ref, q_out, k_out, v_out):
    tm = x_ref.shape[0]
    piece = lambda col, width: x_ref[:, col - MIX_COLS:col - MIX_COLS + width].astype(F32)
    cos = cos_ref[...]
    sin = sin_ref[...]
    lane = lax.broadcasted_iota(jnp.int32, (tm, 128), 1)
    low_half = (lane & 32) == 0
    first = lane < 64

    def rope(x):
        swapped = jnp.where(low_half, pltpu.roll(x, 96, 1), pltpu.roll(x, 32, 1))
        return x * cos + swapped * sin

    for hd in range(C_HEADS):
        x = piece(COL_QN + hd * C_NOPE, C_NOPE)
        q_out[hd, :, 0:C_NOPE] = (_rms(x, gqn_ref[...]) * Q_SCALE).astype(BF16)

    q_pad = jnp.where(lane == C_ROPE, 1.0, 0.0)
    for j in range(C_HEADS // 2):
        x = piece(COL_QR + j * 128, 128)
        x2 = x * x
        s_lo = jnp.sum(jnp.where(first, x2, 0.0), axis=-1, keepdims=True)
        s_hi = jnp.sum(jnp.where(first, 0.0, x2), axis=-1, keepdims=True)
        ms = jnp.where(first, s_lo, s_hi) * (1.0 / C_ROPE)
        r = rope(x * lax.rsqrt(ms + EPS) * gqr_ref[...]) * Q_SCALE
        q_out[2 * j, :, C_NOPE:QK_PAD] = jnp.where(first, r, q_pad).astype(BF16)
        q_out[2 * j + 1, :, C_NOPE:QK_PAD] = jnp.where(first, pltpu.roll(r, 64, 1), q_pad).astype(BF16)

    cn = _rms(piece(COL_CKV, KV_RANK), gkv_ref[...]).astype(BF16)
    kv = jnp.dot(cn, wukv_ref[...], preferred_element_type=F32)
    kr = rope(_rms(piece(COL_KR, 128), gkr_ref[...]))
    kr = jnp.where(first, kr, kpad_ref[...]).astype(BF16)
    ones_col = jnp.where(lane == 0, 1.0, 0.0).astype(BF16)
    for hd in range(C_HEADS):
        kn = kv[:, hd * C_NOPE:(hd + 1) * C_NOPE]
        k_out[hd, :, 0:C_NOPE] = _rms(kn, gkn_ref[...]).astype(BF16)
        k_out[hd, :, C_NOPE:QK_PAD] = kr
        v_out[hd, :, 0:C_V] = kv[:, C_WIDTH + hd * C_V:C_WIDTH + (hd + 1) * C_V].astype(BF16)
        v_out[hd, :, C_V:V_PAD] = ones_col


def _attn_prep(lid, proj, cos128, sin128, gqn, gqr, gkv, wukv, gkn, gkr, kpad, bsz, s_len):
    tm = TM_PREP
    nst = s_len // tm
    head_out = lambda width: pl.BlockSpec((None, C_HEADS, tm, width), lambda i, l: (i // nst, 0, i % nst, 0))
    head_shape = lambda width: jax.ShapeDtypeStruct((bsz, C_HEADS, s_len, width), BF16)
    return pl.pallas_call(
        _prep_kernel,
        grid_spec=pltpu.PrefetchScalarGridSpec(
            num_scalar_prefetch=1,
            grid=(bsz * nst,),
            in_specs=[
                pl.BlockSpec((tm, PREP_COLS), lambda i, l: (i, MIX_COLS // PREP_COLS)),
                pl.BlockSpec((tm, 128), lambda i, l: (i, 0)), pl.BlockSpec((tm, 128), lambda i, l: (i, 0)),
                _layer_spec((1, C_NOPE)), _layer_spec((1, 128)), _layer_spec((1, KV_RANK)),
                _layer_spec((KV_RANK, 2 * C_WIDTH)), _layer_spec((1, C_NOPE)), _layer_spec((1, 128)),
                _layer_spec((1, 128)),
            ],
            out_specs=[head_out(QK_PAD), head_out(QK_PAD), head_out(V_PAD)],
        ),
        out_shape=[head_shape(QK_PAD), head_shape(QK_PAD), head_shape(V_PAD)],
        compiler_params=_params(("arbitrary",)),
        name="attn_prep",
    )(lid, proj, cos128, sin128, gqn, gqr, gkv, wukv, gkn, gkr, kpad)


_NT = (((1,), (1,)), ((), ()))


def _flash_bounded_kernel(q_ref, k_ref, v_ref, o_ref):
    rows = q_ref.shape[0] // FLASH_CHAINS
    s_len = k_ref.shape[0]
    nk = s_len // TK

    def probs(c, j):
        q = q_ref[c * rows:(c + 1) * rows, :]
        s = lax.dot_general(q, k_ref[j * TK:(j + 1) * TK, :], _NT, preferred_element_type=F32)
        return jnp.exp2(s).astype(BF16)

    p = [None] * FLASH_CHAINS
    acc = [jnp.zeros((rows, V_PAD), F32)] * FLASH_CHAINS
    for t in range(-1, nk + FLASH_CHAINS - 1):
        for c in range(FLASH_CHAINS):
            j = t - c
            p_next = probs(c, j + 1) if 0 <= j + 1 < nk else None
            if 0 <= j < nk:
                acc[c] = acc[c] + jnp.dot(p[c], v_ref[j * TK:(j + 1) * TK, :], preferred_element_type=F32)
            if p_next is not None:
                p[c] = p_next
    for c in range(FLASH_CHAINS):
        o_ref[c * rows:(c + 1) * rows, :] = (acc[c][:, 0:C_V] / acc[c][:, C_V:C_V + 1]).astype(o_ref.dtype)


def _flash_online_kernel(q_ref, k_ref, v_ref, o_ref):
    tq = q_ref.shape[0]
    s_len = k_ref.shape[0]
    q = q_ref[...]

    def body(j, carry):
        m, acc = carry
        start = pl.multiple_of(j * TK, TK)
        s = lax.dot_general(q, k_ref[pl.ds(start, TK), :], _NT, preferred_element_type=F32)
        m_new = jnp.maximum(m, jnp.max(s, axis=-1, keepdims=True))
        p = jnp.exp2(s - m_new).astype(BF16)
        acc = jnp.exp2(m - m_new) * acc + jnp.dot(p, v_ref[pl.ds(start, TK), :], preferred_element_type=F32)
        return m_new, acc

    init = (jnp.full((tq, 1), -jnp.inf, F32), jnp.zeros((tq, V_PAD), F32))
    _, acc = lax.fori_loop(0, s_len // TK, body, init)
    o_ref[...] = (acc[:, 0:C_V] / acc[:, C_V:C_V + 1]).astype(o_ref.dtype)


def _flash(q, k, v, bounded):
    bsz, heads, s_len, _ = q.shape
    nq = s_len // TQ

    def call(body, name):
        return pl.pallas_call(
            body,
            grid=(bsz, heads, nq),
            in_specs=[
                pl.BlockSpec((None, None, TQ, QK_PAD), lambda b, hd, i: (b, hd, i, 0)),
                pl.BlockSpec((None, None, s_len, QK_PAD), lambda b, hd, i: (b, hd, 0, 0)),
                pl.BlockSpec((None, None, s_len, V_PAD), lambda b, hd, i: (b, hd, 0, 0)),
            ],
            out_specs=pl.BlockSpec((TQ, C_V), lambda b, hd, i: (b * nq + i, hd)),
            out_shape=jax.ShapeDtypeStruct((bsz * s_len, C_WIDTH), BF16),
            compiler_params=_params(("arbitrary", "arbitrary", "arbitrary")),
            name=name,
        )(q, k, v)

    return lax.cond(bounded,
                    lambda: call(_flash_bounded_kernel, "flash_bounded"),
                    lambda: call(_flash_online_kernel, "flash_online"))


def _mix_kernel(lid_ref, x_ref, xp_ref, xn_ref, o_ref, h_ref, p_ref,
                gsgu_ref, ws_ref, bs_ref, cw_ref, cb_ref, gout_ref, wout_ref, gple_ref, wgate_ref, wpp_ref,
                out_ref, ya_ref, y_ref, *, s_len):
    tm = h_ref.shape[0]
    i = pl.program_id(0)
    piece = lambda col, width: x_ref[:, col:col + width].astype(F32)

    for c in range(tm // CHUNK):
        rows = slice(c * CHUNK, (c + 1) * CHUNK)
        for hd in range(A_HEADS):
            sub = lambda col: x_ref[rows, col + hd * A_HEAD:col + (hd + 1) * A_HEAD].astype(F32)
            vn = _rms(sub(COL_AV), gsgu_ref[hd:hd + 1, :]).astype(BF16)
            s = jnp.dot(ws_ref[hd], vn, preferred_element_type=F32) + bs_ref[hd]
            ya_ref[rows, hd * A_HEAD:(hd + 1) * A_HEAD] = sub(COL_AU) * s * _silu(sub(COL_AZ))
    y_ref[:, 0:A_WIDTH] = _rms(ya_ref[...], gout_ref[:, 0:A_WIDTH]).astype(BF16)

    xc = piece(COL_BC, B_WIDTH) * piece(COL_BH, B_WIDTH)
    row = lax.broadcasted_iota(jnp.int32, xc.shape, 0)
    has_prev = ((i * tm) % s_len != 0).astype(F32)
    has_next = (((i + 1) * tm) % s_len != 0).astype(F32)
    halo = lambda ref, r: ref[r:r + 1, 0:B_WIDTH].astype(F32) * ref[r:r + 1, B_WIDTH:2 * B_WIDTH].astype(F32)
    prev_row = halo(xp_ref, HALO - 1) * has_prev
    next_row = halo(xn_ref, 0) * has_next
    x_prev = jnp.where(row == 0, prev_row, pltpu.roll(xc, 1, 0))
    x_next = jnp.where(row == tm - 1, next_row, pltpu.roll(xc, tm - 1, 0))
    conv = cb_ref[...] + x_prev * cw_ref[0:1, :] + xc * cw_ref[1:2, :] + x_next * cw_ref[2:3, :]
    yb = piece(COL_BB, B_WIDTH) * conv * _silu(piece(COL_BZ, B_WIDTH))
    y_ref[:, A_WIDTH:A_WIDTH + B_WIDTH] = _rms(yb, gout_ref[:, A_WIDTH:A_WIDTH + B_WIDTH]).astype(BF16)

    yc = o_ref[...].astype(F32) * _silu(piece(COL_CZ, C_WIDTH))
    y_ref[:, A_WIDTH + B_WIDTH:] = _rms(yc, gout_ref[:, A_WIDTH + B_WIDTH:]).astype(BF16)

    hm = h_ref[...] + jnp.dot(y_ref[...], wout_ref[...], preferred_element_type=F32)
    hn = _rms(hm, gple_ref[...]).astype(BF16)
    gate = jax.nn.sigmoid(jnp.dot(hn, wgate_ref[...], preferred_element_type=F32))
    pp = jnp.dot(p_ref[...].astype(BF16), wpp_ref[...], preferred_element_type=F32)
    out_ref[...] = hm + gate * pp


def _mix_out(lid, proj, o, h, p, gsgu, ws, bs, cw, cb, gout, wout, gple, wgate, wpp, s_len):
    m = h.shape[0]
    tm = TM_MIX
    nhalo = tm // HALO
    last_halo = m // HALO - 1
    halo_col = COL_BC // (2 * B_WIDTH)
    return pl.pallas_call(
        functools.partial(_mix_kernel, s_len=s_len),
        grid_spec=pltpu.PrefetchScalarGridSpec(
            num_scalar_prefetch=1,
            grid=(m // tm,),
            in_specs=[
                pl.BlockSpec((tm, MIX_COLS), lambda i, l: (i, 0)),
                pl.BlockSpec((HALO, 2 * B_WIDTH), lambda i, l: (jnp.maximum(i * nhalo - 1, 0), halo_col)),
                pl.BlockSpec((HALO, 2 * B_WIDTH), lambda i, l: (jnp.minimum((i + 1) * nhalo, last_halo), halo_col)),
                pl.BlockSpec((tm, C_WIDTH), lambda i, l: (i, 0)),
                pl.BlockSpec((tm, D_MODEL), lambda i, l: (i, 0)),
                pl.BlockSpec((None, tm, PLE_DIM), lambda i, l: (l[0], i, 0)),
                _layer_spec((A_HEADS, A_HEAD)), _layer_spec((A_HEADS, CHUNK, CHUNK)),
                _layer_spec((A_HEADS, CHUNK, A_HEAD)),
                _layer_spec((CONV_W, B_WIDTH)), _layer_spec((1, B_WIDTH)), _layer_spec((1, D_MODEL)),
                _layer_spec((D_MODEL, D_MODEL), 1), _layer_spec((1, D_MODEL)), _layer_spec((D_MODEL, D_MODEL), 1),
                _layer_spec((PLE_DIM, D_MODEL), 1),
            ],
            out_specs=pl.BlockSpec((tm, D_MODEL), lambda i, l: (i, 0)),
            scratch_shapes=[pltpu.VMEM((tm, A_WIDTH), F32), pltpu.VMEM((tm, D_MODEL), BF16)],
        ),
        out_shape=jax.ShapeDtypeStruct((m, D_MODEL), F32),
        compiler_params=pltpu.CompilerParams(dimension_semantics=("arbitrary",), vmem_limit_bytes=VMEM_LIMIT_IN),
        name="mix_out",
    )(lid, proj, proj, proj, o, h, p, gsgu, ws, bs, cw, cb, gout, wout, gple, wgate, wpp)


SRC_AB = 0
SRC_Q = 3 * A_WIDTH + 4 * B_WIDTH
SRC_CKV = SRC_Q + C_HEADS * (C_NOPE + C_ROPE)
SRC_KR = SRC_CKV + KV_RANK
SRC_CZ = SRC_KR + C_ROPE
IN_WIDTH = SRC_CZ + C_WIDTH


def _w_in_kernel(wt_ref, o_ref):
    def put(dst, *pieces):
        rows = [wt_ref[src:src + n, :] for src, n in pieces]
        block = rows[0] if len(rows) == 1 else jnp.concatenate(rows, axis=0)
        o_ref[:, dst:dst + 128] = block.T.astype(BF16)

    for hd in range(C_HEADS):
        put(COL_QN + hd * C_NOPE, (SRC_Q + hd * (C_NOPE + C_ROPE), C_NOPE))
    for j in range(C_HEADS // 2):
        rope = lambda hd: (SRC_Q + hd * (C_NOPE + C_ROPE) + C_NOPE, C_ROPE)
        put(COL_QR + j * 128, rope(2 * j), rope(2 * j + 1))
    for dst, src, width in ((COL_CZ, SRC_CZ, C_WIDTH), (COL_CKV, SRC_CKV, KV_RANK), (COL_AU, SRC_AB, SRC_Q)):
        for c in range(width // 128):
            put(dst + c * 128, (src + c * 128, 128))
    put(COL_KR, (SRC_KR, C_ROPE), (SRC_KR, C_ROPE))
    o_ref[:, IN_USED:IN_PAD] = jnp.zeros((o_ref.shape[0], IN_PAD - IN_USED), BF16)


def _permute_w_in(w):
    depth = w.shape[0]
    return pl.pallas_call(
        _w_in_kernel,
        grid=(depth, D_MODEL // TM_WPREP),
        in_specs=[pl.BlockSpec((None, IN_WIDTH, TM_WPREP), lambda l, i: (l, 0, i))],
        out_specs=pl.BlockSpec((None, TM_WPREP, IN_PAD), lambda l, i: (l, i, 0)),
        out_shape=jax.ShapeDtypeStruct((depth, D_MODEL, IN_PAD), BF16),
        compiler_params=_params(("arbitrary", "arbitrary")),
        name="w_in_prep",
    )(jnp.swapaxes(w, 1, 2))


def _permute_w_ukv(w):
    depth = w.shape[0]
    kv = w.astype(BF16).reshape(depth, KV_RANK, C_HEADS, C_NOPE + C_V)
    kn = kv[..., :C_NOPE].reshape(depth, KV_RANK, C_HEADS * C_NOPE)
    v = kv[..., C_NOPE:].reshape(depth, KV_RANK, C_HEADS * C_V)
    return jnp.concatenate([kn, v], axis=2)


def _score_shift(gqn, gqr, gkn, gkr):
    amax = lambda g: jnp.max(jnp.abs(g), axis=-1)
    bound = Q_SCALE * (C_NOPE * amax(gqn) * amax(gkn) + C_ROPE * amax(gqr) * amax(gkr))
    shift = jnp.ceil(bound * 1.02)
    bounded = shift <= MAX_SHIFT
    return jnp.where(bounded, shift, 0.0), bounded


def kernel(x, p, positions, attn_norm, w_in, sgu_norm, w_spatial, b_spatial, conv_w, conv_b, kv_norm, w_ukv,
           q_nope_norm, q_rope_norm, k_nope_norm, k_rope_norm, out_norm, w_out, ple_norm, w_ple_gate, w_ple_proj):
    bsz, s_len, d = x.shape
    depth = w_in.shape[0]
    m = bsz * s_len

    inv = 1.0 / (ROPE_BASE ** (jnp.arange(0, C_ROPE, 2, dtype=F32) / C_ROPE))
    ang = positions.astype(F32)[..., None] * inv
    cos, sin = jnp.cos(ang), jnp.sin(ang)
    cos128 = jnp.tile(cos, (1, 1, 4)).reshape(m, 128)
    sin128 = jnp.tile(jnp.concatenate([-sin, sin], axis=-1), (1, 1, 2)).reshape(m, 128)

    vec = lambda g: g[:, None, :]
    w_in_p = _permute_w_in(w_in)
    w_ukv_p = _permute_w_ukv(w_ukv)
    shift, bounded = _score_shift(q_nope_norm, q_rope_norm, k_nope_norm, k_rope_norm)
    kpad = jnp.zeros((depth, 1, 128), F32).at[:, 0, C_ROPE].set(-shift)
    gqr2 = vec(jnp.tile(q_rope_norm, (1, 2)))
    gkr2 = vec(jnp.tile(k_rope_norm, (1, 2)))
    bs_full = jnp.broadcast_to(b_spatial[..., None], (depth, A_HEADS, CHUNK, A_HEAD))
    ws_b, wout_b = w_spatial.astype(BF16), w_out.astype(BF16)
    wgate_b, wpp_b = w_ple_gate.astype(BF16), w_ple_proj.astype(BF16)
    p2 = p.reshape(depth, m, PLE_DIM)

    h = x.reshape(m, d)
    for i in range(depth):
        lid = jnp.full((1,), i, jnp.int32)
        proj = _in_proj(lid, h, vec(attn_norm), w_in_p)
        q, k, v = _attn_prep(lid, proj, cos128, sin128, vec(q_nope_norm), gqr2, vec(kv_norm), w_ukv_p,
                             vec(k_nope_norm), gkr2, kpad, bsz, s_len)
        o = _flash(q, k, v, bounded[i])
        h = _mix_out(lid, proj, o, h, p2, sgu_norm, ws_b, bs_full, conv_w, vec(conv_b), vec(out_norm),
                     wout_b, vec(ple_norm), wgate_b, wpp_b, s_len)
    return h.reshape(bsz, s_len, d)
```

```python
import functools
import math

import jax
import jax.numpy as jnp
from jax import lax
from jax.experimental import pallas as pl
from jax.experimental.pallas import tpu as pltpu

F32 = jnp.float32
BF16 = jnp.bfloat16

D_MODEL = 2048
PLE_DIM = 256
EPS = 1e-6
CHUNK = 128
A_HEAD = 128
A_WIDTH = 512
A_HEADS = 4
B_WIDTH = 512
CONV_W = 3
C_WIDTH = 1024
C_V = 128
C_HEADS = 8
C_NOPE = 128
C_ROPE = 64
KV_RANK = 512
ROPE_BASE = 10000.0
QK_PAD = 256
V_PAD = 256
MAX_SHIFT = 50.0

COL_AU, COL_AV, COL_AZ = 0, 512, 1024
COL_BB, COL_BC, COL_BH, COL_BZ = 1536, 2048, 2560, 3072
COL_CZ = 3584
MIX_COLS = 4608
COL_QN = 4608
COL_QR = 5632
COL_CKV = 6144
COL_KR = 6656
IN_USED = 6784
IN_PAD = 6912
PREP_COLS = IN_PAD - MIX_COLS

Q_SCALE = (C_NOPE + C_ROPE) ** -0.5 * math.log2(math.e)

VMEM_LIMIT = 56 * 1024 * 1024

TM_IN, TN_IN = 512, 768
VMEM_LIMIT_IN = 60 * 1024 * 1024
TM_PREP = 512
TM_WPREP = 256
TQ, TK = 2048, 256
FLASH_CHAINS = 2
TM_MIX = 512
HALO = 16


def _rms(x, gain):
    ms = jnp.mean(x * x, axis=-1, keepdims=True)
    return x * lax.rsqrt(ms + EPS) * gain


def _silu(z):
    return z * jax.nn.sigmoid(z)


def _layer_spec(shape, buffers=None):
    mode = {} if buffers is None else {"pipeline_mode": pl.Buffered(buffers)}
    return pl.BlockSpec((None,) + shape, lambda *a: (a[-1][0],) + (0,) * len(shape), **mode)


def _params(semantics):
    return pltpu.CompilerParams(dimension_semantics=semantics, vmem_limit_bytes=VMEM_LIMIT)


def _in_proj_kernel(lid_ref, x_ref, g_ref, w_ref, o_ref, hn_ref):
    hn_ref[...] = _rms(x_ref[...], g_ref[...]).astype(BF16)
    for j in range(IN_PAD // TN_IN):
        cols = slice(j * TN_IN, (j + 1) * TN_IN)
        o_ref[:, cols] = jnp.dot(hn_ref[...], w_ref[:, cols], preferred_element_type=F32).astype(o_ref.dtype)


def _in_proj(lid, h, gain, w):
    m = h.shape[0]
    return pl.pallas_call(
        _in_proj_kernel,
        grid_spec=pltpu.PrefetchScalarGridSpec(
            num_scalar_prefetch=1,
            grid=(m // TM_IN,),
            in_specs=[
                pl.BlockSpec((TM_IN, D_MODEL), lambda i, l: (i, 0)),
                _layer_spec((1, D_MODEL)),
                pl.BlockSpec((None, D_MODEL, IN_PAD), lambda i, l: (l[0], 0, 0), pipeline_mode=pl.Buffered(1)),
            ],
            out_specs=pl.BlockSpec((TM_IN, IN_PAD), lambda i, l: (i, 0)),
            scratch_shapes=[pltpu.VMEM((TM_IN, D_MODEL), BF16)],
        ),
        out_shape=jax.ShapeDtypeStruct((m, IN_PAD), BF16),
        compiler_params=pltpu.CompilerParams(dimension_semantics=("arbitrary",), vmem_limit_bytes=VMEM_LIMIT_IN),
        name="in_proj",
    )(lid, h, gain, w)


def _prep_kernel(lid_ref, x_ref, cos_ref, sin_ref, gqn_ref, gqr_ref, gkv_ref,
                 wukv_ref, gkn_ref, gkr_ref, kpad_ref, q_out, k_out, v_out):
    tm = x_ref.shape[0]
    piece = lambda col, width: x_ref[:, col - MIX_COLS:col - MIX_COLS + width].astype(F32)
    cos = cos_ref[...]
    sin = sin_ref[...]
    lane = lax.broadcasted_iota(jnp.int32, (tm, 128), 1)
    low_half = (lane & 32) == 0
    first = lane < 64

    def rope(x):
        swapped = jnp.where(low_half, pltpu.roll(x, 96, 1), pltpu.roll(x, 32, 1))
        return x * cos + swapped * sin

    for hd in range(C_HEADS):
        x = piece(COL_QN + hd * C_NOPE, C_NOPE)
        q_out[hd, :, 0:C_NOPE] = (_rms(x, gqn_ref[...]) * Q_SCALE).astype(BF16)

    q_pad = jnp.where(lane == C_ROPE, 1.0, 0.0)
    for j in range(C_HEADS // 2):
        x = piece(COL_QR + j * 128, 128)
        x2 = x * x
        s_lo = jnp.sum(jnp.where(first, x2, 0.0), axis=-1, keepdims=True)
        s_hi = jnp.sum(jnp.where(first, 0.0, x2), axis=-1, keepdims=True)
        ms = jnp.where(first, s_lo, s_hi) * (1.0 / C_ROPE)
        r = rope(x * lax.rsqrt(ms + EPS) * gqr_ref[...]) * Q_SCALE
        q_out[2 * j, :, C_NOPE:QK_PAD] = jnp.where(first, r, q_pad).astype(BF16)
        q_out[2 * j + 1, :, C_NOPE:QK_PAD] = jnp.where(first, pltpu.roll(r, 64, 1), q_pad).astype(BF16)

    cn = _rms(piece(COL_CKV, KV_RANK), gkv_ref[...]).astype(BF16)
    kv = jnp.dot(cn, wukv_ref[...], preferred_element_type=F32)
    kr = rope(_rms(piece(COL_KR, 128), gkr_ref[...]))
    kr = jnp.where(first, kr, kpad_ref[...]).astype(BF16)
    ones_col = jnp.where(lane == 0, 1.0, 0.0).astype(BF16)
    for hd in range(C_HEADS):
        base = hd * (C_NOPE + C_V)
        kn = kv[:, base:base + C_NOPE]
        k_out[hd, :, 0:C_NOPE] = _rms(kn, gkn_ref[...]).astype(BF16)
        k_out[hd, :, C_NOPE:QK_PAD] = kr
        v_out[hd, :, 0:C_V] = kv[:, base + C_NOPE:base + C_NOPE + C_V].astype(BF16)
        v_out[hd, :, C_V:V_PAD] = ones_col


def _attn_prep(lid, proj, cos128, sin128, gqn, gqr, gkv, wukv, gkn, gkr, kpad, bsz, s_len):
    tm = TM_PREP
    nst = s_len // tm
    head_out = lambda width: pl.BlockSpec((None, C_HEADS, tm, width), lambda i, l: (i // nst, 0, i % nst, 0))
    head_shape = lambda width: jax.ShapeDtypeStruct((bsz, C_HEADS, s_len, width), BF16)
    return pl.pallas_call(
        _prep_kernel,
        grid_spec=pltpu.PrefetchScalarGridSpec(
            num_scalar_prefetch=1,
            grid=(bsz * nst,),
            in_specs=[
                pl.BlockSpec((tm, PREP_COLS), lambda i, l: (i, MIX_COLS // PREP_COLS)),
                pl.BlockSpec((tm, 128), lambda i, l: (i, 0)), pl.BlockSpec((tm, 128), lambda i, l: (i, 0)),
                _layer_spec((1, C_NOPE)), _layer_spec((1, 128)), _layer_spec((1, KV_RANK)),
                _layer_spec((KV_RANK, 2 * C_WIDTH)), _layer_spec((1, C_NOPE)), _layer_spec((1, 128)),
                _layer_spec((1, 128)),
            ],
            out_specs=[head_out(QK_PAD), head_out(QK_PAD), head_out(V_PAD)],
        ),
        out_shape=[head_shape(QK_PAD), head_shape(QK_PAD), head_shape(V_PAD)],
        compiler_params=_params(("arbitrary",)),
        name="attn_prep",
    )(lid, proj, cos128, sin128, gqn, gqr, gkv, wukv, gkn, gkr, kpad)


_NT = (((1,), (1,)), ((), ()))


def _flash_bounded_kernel(q_ref, k_ref, v_ref, o_ref):
    rows = q_ref.shape[0] // FLASH_CHAINS
    s_len = k_ref.shape[0]
    nk = s_len // TK

    def probs(c, j):
        q = q_ref[c * rows:(c + 1) * rows, :]
        s = lax.dot_general(q, k_ref[j * TK:(j + 1) * TK, :], _NT, preferred_element_type=F32)
        return jnp.exp2(s).astype(BF16)

    p = [None] * FLASH_CHAINS
    acc = [jnp.zeros((rows, V_PAD), F32)] * FLASH_CHAINS
    for t in range(-1, nk + FLASH_CHAINS - 1):
        for c in range(FLASH_CHAINS):
            j = t - c
            p_next = probs(c, j + 1) if 0 <= j + 1 < nk else None
            if 0 <= j < nk:
                acc[c] = acc[c] + jnp.dot(p[c], v_ref[j * TK:(j + 1) * TK, :], preferred_element_type=F32)
            if p_next is not None:
                p[c] = p_next
    for c in range(FLASH_CHAINS):
        o_ref[c * rows:(c + 1) * rows, :] = (acc[c][:, 0:C_V] / acc[c][:, C_V:C_V + 1]).astype(o_ref.dtype)


def _flash_online_kernel(q_ref, k_ref, v_ref, o_ref):
    tq = q_ref.shape[0]
    s_len = k_ref.shape[0]
    q = q_ref[...]

    def body(j, carry):
        m, acc = carry
        start = pl.multiple_of(j * TK, TK)
        s = lax.dot_general(q, k_ref[pl.ds(start, TK), :], _NT, preferred_element_type=F32)
        m_new = jnp.maximum(m, jnp.max(s, axis=-1, keepdims=True))
        p = jnp.exp2(s - m_new).astype(BF16)
        acc = jnp.exp2(m - m_new) * acc + jnp.dot(p, v_ref[pl.ds(start, TK), :], preferred_element_type=F32)
        return m_new, acc

    init = (jnp.full((tq, 1), -jnp.inf, F32), jnp.zeros((tq, V_PAD), F32))
    _, acc = lax.fori_loop(0, s_len // TK, body, init)
    o_ref[...] = (acc[:, 0:C_V] / acc[:, C_V:C_V + 1]).astype(o_ref.dtype)


def _flash(q, k, v, bounded):
    bsz, heads, s_len, _ = q.shape
    nq = s_len // TQ

    def call(body, name):
        return pl.pallas_call(
            body,
            grid=(bsz, heads, nq),
            in_specs=[
                pl.BlockSpec((None, None, TQ, QK_PAD), lambda b, hd, i: (b, hd, i, 0)),
                pl.BlockSpec((None, None, s_len, QK_PAD), lambda b, hd, i: (b, hd, 0, 0)),
                pl.BlockSpec((None, None, s_len, V_PAD), lambda b, hd, i: (b, hd, 0, 0)),
            ],
            out_specs=pl.BlockSpec((TQ, C_V), lambda b, hd, i: (b * nq + i, hd)),
            out_shape=jax.ShapeDtypeStruct((bsz * s_len, C_WIDTH), BF16),
            compiler_params=_params(("arbitrary", "arbitrary", "arbitrary")),
            name=name,
        )(q, k, v)

    return lax.cond(bounded,
                    lambda: call(_flash_bounded_kernel, "flash_bounded"),
                    lambda: call(_flash_online_kernel, "flash_online"))


def _mix_kernel(lid_ref, x_ref, xp_ref, xn_ref, o_ref, h_ref, p_ref,
                gsgu_ref, ws_ref, bs_ref, cw_ref, cb_ref, gout_ref, wout_ref, gple_ref, wgate_ref, wpp_ref,
                out_ref, ya_ref, y_ref, *, s_len):
    tm = h_ref.shape[0]
    i = pl.program_id(0)
    piece = lambda col, width: x_ref[:, col:col + width].astype(F32)

    for c in range(tm // CHUNK):
        rows = slice(c * CHUNK, (c + 1) * CHUNK)
        for hd in range(A_HEADS):
            sub = lambda col: x_ref[rows, col + hd * A_HEAD:col + (hd + 1) * A_HEAD].astype(F32)
            vn = _rms(sub(COL_AV), gsgu_ref[hd:hd + 1, :]).astype(BF16)
            s = jnp.dot(ws_ref[hd], vn, preferred_element_type=F32) + bs_ref[hd]
            ya_ref[rows, hd * A_HEAD:(hd + 1) * A_HEAD] = sub(COL_AU) * s * _silu(sub(COL_AZ))
    y_ref[:, 0:A_WIDTH] = _rms(ya_ref[...], gout_ref[:, 0:A_WIDTH]).astype(BF16)

    xc = piece(COL_BC, B_WIDTH) * piece(COL_BH, B_WIDTH)
    row = lax.broadcasted_iota(jnp.int32, xc.shape, 0)
    has_prev = ((i * tm) % s_len != 0).astype(F32)
    has_next = (((i + 1) * tm) % s_len != 0).astype(F32)
    halo = lambda ref, r: ref[r:r + 1, 0:B_WIDTH].astype(F32) * ref[r:r + 1, B_WIDTH:2 * B_WIDTH].astype(F32)
    prev_row = halo(xp_ref, HALO - 1) * has_prev
    next_row = halo(xn_ref, 0) * has_next
    x_prev = jnp.where(row == 0, prev_row, pltpu.roll(xc, 1, 0))
    x_next = jnp.where(row == tm - 1, next_row, pltpu.roll(xc, tm - 1, 0))
    conv = cb_ref[...] + x_prev * cw_ref[0:1, :] + xc * cw_ref[1:2, :] + x_next * cw_ref[2:3, :]
    yb = piece(COL_BB, B_WIDTH) * conv * _silu(piece(COL_BZ, B_WIDTH))
    y_ref[:, A_WIDTH:A_WIDTH + B_WIDTH] = _rms(yb, gout_ref[:, A_WIDTH:A_WIDTH + B_WIDTH]).astype(BF16)

    yc = o_ref[...].astype(F32) * _silu(piece(COL_CZ, C_WIDTH))
    y_ref[:, A_WIDTH + B_WIDTH:] = _rms(yc, gout_ref[:, A_WIDTH + B_WIDTH:]).astype(BF16)

    hm = h_ref[...] + jnp.dot(y_ref[...], wout_ref[...], preferred_element_type=F32)
    hn = _rms(hm, gple_ref[...]).astype(BF16)
    gate = jax.nn.sigmoid(jnp.dot(hn, wgate_ref[...], preferred_element_type=F32))
    pp = jnp.dot(p_ref[...].astype(BF16), wpp_ref[...], preferred_element_type=F32)
    out_ref[...] = hm + gate * pp


def _mix_out(lid, proj, o, h, p, gsgu, ws, bs, cw, cb, gout, wout, gple, wgate, wpp, s_len):
    m = h.shape[0]
    tm = TM_MIX
    nhalo = tm // HALO
    last_halo = m // HALO - 1
    halo_col = COL_BC // (2 * B_WIDTH)
    return pl.pallas_call(
        functools.partial(_mix_kernel, s_len=s_len),
        grid_spec=pltpu.PrefetchScalarGridSpec(
            num_scalar_prefetch=1,
            grid=(m // tm,),
            in_specs=[
                pl.BlockSpec((tm, MIX_COLS), lambda i, l: (i, 0)),
                pl.BlockSpec((HALO, 2 * B_WIDTH), lambda i, l: (jnp.maximum(i * nhalo - 1, 0), halo_col)),
                pl.BlockSpec((HALO, 2 * B_WIDTH), lambda i, l: (jnp.minimum((i + 1) * nhalo, last_halo), halo_col)),
                pl.BlockSpec((tm, C_WIDTH), lambda i, l: (i, 0)),
                pl.BlockSpec((tm, D_MODEL), lambda i, l: (i, 0)),
                pl.BlockSpec((None, tm, PLE_DIM), lambda i, l: (l[0], i, 0)),
                _layer_spec((A_HEADS, A_HEAD)), _layer_spec((A_HEADS, CHUNK, CHUNK)),
                _layer_spec((A_HEADS, CHUNK, A_HEAD)),
                _layer_spec((CONV_W, B_WIDTH)), _layer_spec((1, B_WIDTH)), _layer_spec((1, D_MODEL)),
                _layer_spec((D_MODEL, D_MODEL), 1), _layer_spec((1, D_MODEL)), _layer_spec((D_MODEL, D_MODEL), 1),
                _layer_spec((PLE_DIM, D_MODEL), 1),
            ],
            out_specs=pl.BlockSpec((tm, D_MODEL), lambda i, l: (i, 0)),
            scratch_shapes=[pltpu.VMEM((tm, A_WIDTH), F32), pltpu.VMEM((tm, D_MODEL), BF16)],
        ),
        out_shape=jax.ShapeDtypeStruct((m, D_MODEL), F32),
        compiler_params=pltpu.CompilerParams(dimension_semantics=("arbitrary",), vmem_limit_bytes=VMEM_LIMIT_IN),
        name="mix_out",
    )(lid, proj, proj, proj, o, h, p, gsgu, ws, bs, cw, cb, gout, wout, gple, wgate, wpp)


SRC_AB = 0
SRC_Q = 3 * A_WIDTH + 4 * B_WIDTH
SRC_CKV = SRC_Q + C_HEADS * (C_NOPE + C_ROPE)
SRC_KR = SRC_CKV + KV_RANK
SRC_CZ = SRC_KR + C_ROPE
IN_WIDTH = SRC_CZ + C_WIDTH


def _w_in_kernel(wt_ref, o_ref):
    def put(dst, *pieces):
        rows = [wt_ref[src:src + n, :] for src, n in pieces]
        block = rows[0] if len(rows) == 1 else jnp.concatenate(rows, axis=0)
        o_ref[:, dst:dst + 128] = block.T.astype(BF16)

    for hd in range(C_HEADS):
        put(COL_QN + hd * C_NOPE, (SRC_Q + hd * (C_NOPE + C_ROPE), C_NOPE))
    for j in range(C_HEADS // 2):
        rope = lambda hd: (SRC_Q + hd * (C_NOPE + C_ROPE) + C_NOPE, C_ROPE)
        put(COL_QR + j * 128, rope(2 * j), rope(2 * j + 1))
    for dst, src, width in ((COL_CZ, SRC_CZ, C_WIDTH), (COL_CKV, SRC_CKV, KV_RANK), (COL_AU, SRC_AB, SRC_Q)):
        for c in range(width // 128):
            put(dst + c * 128, (src + c * 128, 128))
    put(COL_KR, (SRC_KR, C_ROPE), (SRC_KR, C_ROPE))
    o_ref[:, IN_USED:IN_PAD] = jnp.zeros((o_ref.shape[0], IN_PAD - IN_USED), BF16)


def _permute_w_in(w):
    depth = w.shape[0]
    return pl.pallas_call(
        _w_in_kernel,
        grid=(depth, D_MODEL // TM_WPREP),
        in_specs=[pl.BlockSpec((None, IN_WIDTH, TM_WPREP), lambda l, i: (l, 0, i))],
        out_specs=pl.BlockSpec((None, TM_WPREP, IN_PAD), lambda l, i: (l, i, 0)),
        out_shape=jax.ShapeDtypeStruct((depth, D_MODEL, IN_PAD), BF16),
        compiler_params=_params(("arbitrary", "arbitrary")),
        name="w_in_prep",
    )(jnp.swapaxes(w, 1, 2))


def _score_shift(gqn, gqr, gkn, gkr):
    amax = lambda g: jnp.max(jnp.abs(g), axis=-1)
    bound = Q_SCALE * (C_NOPE * amax(gqn) * amax(gkn) + C_ROPE * amax(gqr) * amax(gkr))
    shift = jnp.ceil(bound * 1.02)
    bounded = shift <= MAX_SHIFT
    return jnp.where(bounded, shift, 0.0), bounded


def kernel(x, p, positions, attn_norm, w_in, sgu_norm, w_spatial, b_spatial, conv_w, conv_b, kv_norm, w_ukv,
           q_nope_norm, q_rope_norm, k_nope_norm, k_rope_norm, out_norm, w_out, ple_norm, w_ple_gate, w_ple_proj):
    bsz, s_len, d = x.shape
    depth = w_in.shape[0]
    m = bsz * s_len

    inv = 1.0 / (ROPE_BASE ** (jnp.arange(0, C_ROPE, 2, dtype=F32) / C_ROPE))
    ang = positions.astype(F32).reshape(m, 1) * jnp.tile(inv, 4)[None, :]
    sign = jnp.tile(jnp.concatenate([-jnp.ones(C_ROPE // 2, F32), jnp.ones(C_ROPE // 2, F32)]), 2)
    cos128 = jnp.cos(ang)
    sin128 = jnp.sin(ang) * sign[None, :]

    vec = lambda g: g[:, None, :]
    w_in_p = _permute_w_in(w_in)
    w_ukv_b = w_ukv.astype(BF16)
    shift, bounded = _score_shift(q_nope_norm, q_rope_norm, k_nope_norm, k_rope_norm)
    kpad = jnp.zeros((depth, 1, 128), F32).at[:, 0, C_ROPE].set(-shift)
    gqr2 = vec(jnp.tile(q_rope_norm, (1, 2)))
    gkr2 = vec(jnp.tile(k_rope_norm, (1, 2)))
    bs_full = jnp.broadcast_to(b_spatial[..., None], (depth, A_HEADS, CHUNK, A_HEAD))
    ws_b, wout_b = w_spatial.astype(BF16), w_out.astype(BF16)
    wgate_b, wpp_b = w_ple_gate.astype(BF16), w_ple_proj.astype(BF16)
    p2 = p.reshape(depth, m, PLE_DIM)

    h = x.reshape(m, d)
    for i in range(depth):
        lid = jnp.full((1,), i, jnp.int32)
        proj = _in_proj(lid, h, vec(attn_norm), w_in_p)
        q, k, v = _attn_prep(lid, proj, cos128, sin128, vec(q_nope_norm), gqr2, vec(kv_norm), w_ukv_b,
                             vec(k_nope_norm), gkr2, kpad, bsz, s_len)
        o = _flash(q, k, v, bounded[i])
        h = _mix_out(lid, proj, o, h, p2, sgu_norm, ws_b, bs_full, conv_w, vec(conv_b), vec(out_norm),
                     wout_b, vec(ple_norm), wgate_b, wpp_b, s_len)
    return h.reshape(bsz, s_len, d)
```

```python
import functools
import math

import jax
import jax.numpy as jnp
from jax import lax
from jax.experimental import pallas as pl
from jax.experimental.pallas import tpu as pltpu

F32 = jnp.float32
BF16 = jnp.bfloat16

D_MODEL = 2048
PLE_DIM = 256
EPS = 1e-6
CHUNK = 128
A_HEAD = 128
A_WIDTH = 512
A_HEADS = 4
B_WIDTH = 512
CONV_W = 3
C_WIDTH = 1024
C_V = 128
C_HEADS = 8
C_NOPE = 128
C_ROPE = 64
KV_RANK = 512
ROPE_BASE = 10000.0
QK_PAD = 256
V_PAD = 256
MAX_SHIFT = 50.0

COL_AU, COL_AV, COL_AZ = 0, 512, 1024
COL_BB, COL_BC, COL_BH, COL_BZ = 1536, 2048, 2560, 3072
COL_CZ = 3584
MIX_COLS = 4608
COL_QN = 4608
COL_QR = 5632
COL_CKV = 6144
COL_KR = 6656
IN_USED = 6784
IN_PAD = 6912
PREP_COLS = IN_PAD - MIX_COLS

Q_SCALE = (C_NOPE + C_ROPE) ** -0.5 * math.log2(math.e)

VMEM_LIMIT = 56 * 1024 * 1024

TM_IN, TN_IN = 256, 768
VMEM_LIMIT_IN = 60 * 1024 * 1024
TM_WPREP = 256
TQ, TK = 2048, 256
FLASH_CHAINS = 2
TM_MIX = 512
HALO = 16


def _rms(x, gain):
    ms = jnp.mean(x * x, axis=-1, keepdims=True)
    return x * lax.rsqrt(ms + EPS) * gain


def _silu(z):
    return z * jax.nn.sigmoid(z)


def _layer_spec(shape, buffers=None):
    mode = {} if buffers is None else {"pipeline_mode": pl.Buffered(buffers)}
    return pl.BlockSpec((None,) + shape, lambda *a: (a[-1][0],) + (0,) * len(shape), **mode)


def _params(semantics):
    return pltpu.CompilerParams(dimension_semantics=semantics, vmem_limit_bytes=VMEM_LIMIT)


def _in_proj_kernel(lid_ref, x_ref, g_ref, w_ref, cos_ref, sin_ref, gqn_ref, gqr_ref, gkv_ref,
                    wukv_ref, gkn_ref, gkr_ref, kpad_ref, o_ref, q_out, k_out, v_out, hn_ref):
    tm = x_ref.shape[0]
    hn_ref[...] = _rms(x_ref[...], g_ref[...]).astype(BF16)

    def project(col):
        return jnp.dot(hn_ref[...], w_ref[:, col:col + TN_IN], preferred_element_type=F32)

    def mix_chunk(j):
        o_ref[:, j * TN_IN:(j + 1) * TN_IN] = project(j * TN_IN).astype(o_ref.dtype)

    n_mix = MIX_COLS // TN_IN
    attn = [project(MIX_COLS + k * TN_IN) for k in range(PREP_COLS // TN_IN)]

    def piece(col, width):
        k, off = divmod(col - MIX_COLS, TN_IN)
        assert off + width <= TN_IN
        return attn[k][:, off:off + width]

    cos = cos_ref[...]
    sin = sin_ref[...]
    lane = lax.broadcasted_iota(jnp.int32, (tm, 128), 1)
    low_half = (lane & 32) == 0
    first = lane < 64

    def rope(x):
        swapped = jnp.where(low_half, pltpu.roll(x, 96, 1), pltpu.roll(x, 32, 1))
        return x * cos + swapped * sin

    cn = _rms(piece(COL_CKV, KV_RANK), gkv_ref[...]).astype(BF16)
    mix_chunk(0)
    mix_chunk(1)
    kv = jnp.dot(cn, wukv_ref[...], preferred_element_type=F32)
    for j in range(2, n_mix):
        mix_chunk(j)

    for hd in range(C_HEADS):
        x = piece(COL_QN + hd * C_NOPE, C_NOPE)
        q_out[hd, :, 0:C_NOPE] = (_rms(x, gqn_ref[...]) * Q_SCALE).astype(BF16)

    q_pad = jnp.where(lane == C_ROPE, 1.0, 0.0)
    for j in range(C_HEADS // 2):
        x = piece(COL_QR + j * 128, 128)
        x2 = x * x
        s_lo = jnp.sum(jnp.where(first, x2, 0.0), axis=-1, keepdims=True)
        s_hi = jnp.sum(jnp.where(first, 0.0, x2), axis=-1, keepdims=True)
        ms = jnp.where(first, s_lo, s_hi) * (1.0 / C_ROPE)
        r = rope(x * lax.rsqrt(ms + EPS) * gqr_ref[...]) * Q_SCALE
        q_out[2 * j, :, C_NOPE:QK_PAD] = jnp.where(first, r, q_pad).astype(BF16)
        q_out[2 * j + 1, :, C_NOPE:QK_PAD] = jnp.where(first, pltpu.roll(r, 64, 1), q_pad).astype(BF16)

    kr = rope(_rms(piece(COL_KR, 128), gkr_ref[...]))
    kr = jnp.where(first, kr, kpad_ref[...]).astype(BF16)
    ones_col = jnp.where(lane == 0, 1.0, 0.0).astype(BF16)
    for hd in range(C_HEADS):
        base = hd * (C_NOPE + C_V)
        kn = kv[:, base:base + C_NOPE]
        k_out[hd, :, 0:C_NOPE] = _rms(kn, gkn_ref[...]).astype(BF16)
        k_out[hd, :, C_NOPE:QK_PAD] = kr
        v_out[hd, :, 0:C_V] = kv[:, base + C_NOPE:base + C_NOPE + C_V].astype(BF16)
        v_out[hd, :, C_V:V_PAD] = ones_col


def _in_proj(lid, h, gain, w, cos128, sin128, gqn, gqr, gkv, wukv, gkn, gkr, kpad, bsz, s_len):
    m = h.shape[0]
    tm = TM_IN
    nst = s_len // tm
    rows = lambda width: pl.BlockSpec((tm, width), lambda i, l: (i, 0))
    head_out = lambda width: pl.BlockSpec((None, C_HEADS, tm, width), lambda i, l: (i // nst, 0, i % nst, 0))
    head_shape = lambda width: jax.ShapeDtypeStruct((bsz, C_HEADS, s_len, width), BF16)
    return pl.pallas_call(
        _in_proj_kernel,
        grid_spec=pltpu.PrefetchScalarGridSpec(
            num_scalar_prefetch=1,
            grid=(m // tm,),
            in_specs=[
                rows(D_MODEL), _layer_spec((1, D_MODEL)), _layer_spec((D_MODEL, IN_PAD), 1),
                rows(128), rows(128),
                _layer_spec((1, C_NOPE)), _layer_spec((1, 128)), _layer_spec((1, KV_RANK)),
                _layer_spec((KV_RANK, 2 * C_WIDTH), 1), _layer_spec((1, C_NOPE)), _layer_spec((1, 128)),
                _layer_spec((1, 128)),
            ],
            out_specs=[rows(MIX_COLS), head_out(QK_PAD), head_out(QK_PAD), head_out(V_PAD)],
            scratch_shapes=[pltpu.VMEM((tm, D_MODEL), BF16)],
        ),
        out_shape=[jax.ShapeDtypeStruct((m, MIX_COLS), BF16),
                   head_shape(QK_PAD), head_shape(QK_PAD), head_shape(V_PAD)],
        compiler_params=pltpu.CompilerParams(dimension_semantics=("arbitrary",), vmem_limit_bytes=VMEM_LIMIT_IN),
        name="in_proj",
    )(lid, h, gain, w, cos128, sin128, gqn, gqr, gkv, wukv, gkn, gkr, kpad)


_NT = (((1,), (1,)), ((), ()))


def _flash_bounded_kernel(q_ref, k_ref, v_ref, o_ref):
    rows = q_ref.shape[0] // FLASH_CHAINS
    s_len = k_ref.shape[0]
    nk = s_len // TK

    def probs(c, j):
        q = q_ref[c * rows:(c + 1) * rows, :]
        s = lax.dot_general(q, k_ref[j * TK:(j + 1) * TK, :], _NT, preferred_element_type=F32)
        return jnp.exp2(s).astype(BF16)

    p = [None] * FLASH_CHAINS
    acc = [jnp.zeros((rows, V_PAD), F32)] * FLASH_CHAINS
    for t in range(-1, nk + FLASH_CHAINS - 1):
        for c in range(FLASH_CHAINS):
            j = t - c
            p_next = probs(c, j + 1) if 0 <= j + 1 < nk else None
            if 0 <= j < nk:
                acc[c] = acc[c] + jnp.dot(p[c], v_ref[j * TK:(j + 1) * TK, :], preferred_element_type=F32)
            if p_next is not None:
                p[c] = p_next
    for c in range(FLASH_CHAINS):
        o_ref[c * rows:(c + 1) * rows, :] = (acc[c][:, 0:C_V] / acc[c][:, C_V:C_V + 1]).astype(o_ref.dtype)


def _flash_online_kernel(q_ref, k_ref, v_ref, o_ref):
    tq = q_ref.shape[0]
    s_len = k_ref.shape[0]
    q = q_ref[...]

    def body(j, carry):
        m, acc = carry
        start = pl.multiple_of(j * TK, TK)
        s = lax.dot_general(q, k_ref[pl.ds(start, TK), :], _NT, preferred_element_type=F32)
        m_new = jnp.maximum(m, jnp.max(s, axis=-1, keepdims=True))
        p = jnp.exp2(s - m_new).astype(BF16)
        acc = jnp.exp2(m - m_new) * acc + jnp.dot(p, v_ref[pl.ds(start, TK), :], preferred_element_type=F32)
        return m_new, acc

    init = (jnp.full((tq, 1), -jnp.inf, F32), jnp.zeros((tq, V_PAD), F32))
    _, acc = lax.fori_loop(0, s_len // TK, body, init)
    o_ref[...] = (acc[:, 0:C_V] / acc[:, C_V:C_V + 1]).astype(o_ref.dtype)


def _flash(q, k, v, bounded):
    bsz, heads, s_len, _ = q.shape
    nq = s_len // TQ

    def call(body, name):
        return pl.pallas_call(
            body,
            grid=(bsz, heads, nq),
            in_specs=[
                pl.BlockSpec((None, None, TQ, QK_PAD), lambda b, hd, i: (b, hd, i, 0)),
                pl.BlockSpec((None, None, s_len, QK_PAD), lambda b, hd, i: (b, hd, 0, 0)),
                pl.BlockSpec((None, None, s_len, V_PAD), lambda b, hd, i: (b, hd, 0, 0)),
            ],
            out_specs=pl.BlockSpec((TQ, C_V), lambda b, hd, i: (b * nq + i, hd)),
            out_shape=jax.ShapeDtypeStruct((bsz * s_len, C_WIDTH), BF16),
            compiler_params=_params(("arbitrary", "arbitrary", "arbitrary")),
            name=name,
        )(q, k, v)

    return lax.cond(bounded,
                    lambda: call(_flash_bounded_kernel, "flash_bounded"),
                    lambda: call(_flash_online_kernel, "flash_online"))


def _mix_kernel(lid_ref, x_ref, xp_ref, xn_ref, o_ref, h_ref, p_ref,
                gsgu_ref, ws_ref, bs_ref, cw_ref, cb_ref, gout_ref, wout_ref, gple_ref, wgate_ref, wpp_ref,
                out_ref, ya_ref, y_ref, *, s_len):
    tm = h_ref.shape[0]
    i = pl.program_id(0)
    piece = lambda col, width: x_ref[:, col:col + width].astype(F32)

    for c in range(tm // CHUNK):
        rows = slice(c * CHUNK, (c + 1) * CHUNK)
        for hd in range(A_HEADS):
            sub = lambda col: x_ref[rows, col + hd * A_HEAD:col + (hd + 1) * A_HEAD].astype(F32)
            vn = _rms(sub(COL_AV), gsgu_ref[hd:hd + 1, :]).astype(BF16)
            s = jnp.dot(ws_ref[hd], vn, preferred_element_type=F32) + bs_ref[hd]
            ya_ref[rows, hd * A_HEAD:(hd + 1) * A_HEAD] = sub(COL_AU) * s * _silu(sub(COL_AZ))
    y_ref[:, 0:A_WIDTH] = _rms(ya_ref[...], gout_ref[:, 0:A_WIDTH]).astype(BF16)

    xc = piece(COL_BC, B_WIDTH) * piece(COL_BH, B_WIDTH)
    row = lax.broadcasted_iota(jnp.int32, xc.shape, 0)
    has_prev = ((i * tm) % s_len != 0).astype(F32)
    has_next = (((i + 1) * tm) % s_len != 0).astype(F32)
    halo = lambda ref, r: ref[r:r + 1, 0:B_WIDTH].astype(F32) * ref[r:r + 1, B_WIDTH:2 * B_WIDTH].astype(F32)
    prev_row = halo(xp_ref, HALO - 1) * has_prev
    next_row = halo(xn_ref, 0) * has_next
    x_prev = jnp.where(row == 0, prev_row, pltpu.roll(xc, 1, 0))
    x_next = jnp.where(row == tm - 1, next_row, pltpu.roll(xc, tm - 1, 0))
    conv = cb_ref[...] + x_prev * cw_ref[0:1, :] + xc * cw_ref[1:2, :] + x_next * cw_ref[2:3, :]
    yb = piece(COL_BB, B_WIDTH) * conv * _silu(piece(COL_BZ, B_WIDTH))
    y_ref[:, A_WIDTH:A_WIDTH + B_WIDTH] = _rms(yb, gout_ref[:, A_WIDTH:A_WIDTH + B_WIDTH]).astype(BF16)

    yc = o_ref[...].astype(F32) * _silu(piece(COL_CZ, C_WIDTH))
    y_ref[:, A_WIDTH + B_WIDTH:] = _rms(yc, gout_ref[:, A_WIDTH + B_WIDTH:]).astype(BF16)

    hm = h_ref[...] + jnp.dot(y_ref[...], wout_ref[...], preferred_element_type=F32)
    hn = _rms(hm, gple_ref[...]).astype(BF16)
    gate = jax.nn.sigmoid(jnp.dot(hn, wgate_ref[...], preferred_element_type=F32))
    pp = jnp.dot(p_ref[...].astype(BF16), wpp_ref[...], preferred_element_type=F32)
    out_ref[...] = hm + gate * pp


def _mix_out(lid, proj, o, h, p, gsgu, ws, bs, cw, cb, gout, wout, gple, wgate, wpp, s_len):
    m = h.shape[0]
    tm = TM_MIX
    nhalo = tm // HALO
    last_halo = m // HALO - 1
    halo_col = COL_BC // (2 * B_WIDTH)
    return pl.pallas_call(
        functools.partial(_mix_kernel, s_len=s_len),
        grid_spec=pltpu.PrefetchScalarGridSpec(
            num_scalar_prefetch=1,
            grid=(m // tm,),
            in_specs=[
                pl.BlockSpec((tm, MIX_COLS), lambda i, l: (i, 0)),
                pl.BlockSpec((HALO, 2 * B_WIDTH), lambda i, l: (jnp.maximum(i * nhalo - 1, 0), halo_col)),
                pl.BlockSpec((HALO, 2 * B_WIDTH), lambda i, l: (jnp.minimum((i + 1) * nhalo, last_halo), halo_col)),
                pl.BlockSpec((tm, C_WIDTH), lambda i, l: (i, 0)),
                pl.BlockSpec((tm, D_MODEL), lambda i, l: (i, 0)),
                pl.BlockSpec((None, tm, PLE_DIM), lambda i, l: (l[0], i, 0)),
                _layer_spec((A_HEADS, A_HEAD)), _layer_spec((A_HEADS, CHUNK, CHUNK)),
                _layer_spec((A_HEADS, CHUNK, A_HEAD)),
                _layer_spec((CONV_W, B_WIDTH)), _layer_spec((1, B_WIDTH)), _layer_spec((1, D_MODEL)),
                _layer_spec((D_MODEL, D_MODEL), 1), _layer_spec((1, D_MODEL)), _layer_spec((D_MODEL, D_MODEL), 1),
                _layer_spec((PLE_DIM, D_MODEL), 1),
            ],
            out_specs=pl.BlockSpec((tm, D_MODEL), lambda i, l: (i, 0)),
            scratch_shapes=[pltpu.VMEM((tm, A_WIDTH), F32), pltpu.VMEM((tm, D_MODEL), BF16)],
        ),
        out_shape=jax.ShapeDtypeStruct((m, D_MODEL), F32),
        compiler_params=pltpu.CompilerParams(dimension_semantics=("arbitrary",), vmem_limit_bytes=VMEM_LIMIT_IN),
        name="mix_out",
    )(lid, proj, proj, proj, o, h, p, gsgu, ws, bs, cw, cb, gout, wout, gple, wgate, wpp)


SRC_AB = 0
SRC_Q = 3 * A_WIDTH + 4 * B_WIDTH
SRC_CKV = SRC_Q + C_HEADS * (C_NOPE + C_ROPE)
SRC_KR = SRC_CKV + KV_RANK
SRC_CZ = SRC_KR + C_ROPE
IN_WIDTH = SRC_CZ + C_WIDTH


def _w_in_kernel(wt_ref, o_ref):
    def put(dst, *pieces):
        rows = [wt_ref[src:src + n, :] for src, n in pieces]
        block = rows[0] if len(rows) == 1 else jnp.concatenate(rows, axis=0)
        o_ref[:, dst:dst + 128] = block.T.astype(BF16)

    for hd in range(C_HEADS):
        put(COL_QN + hd * C_NOPE, (SRC_Q + hd * (C_NOPE + C_ROPE), C_NOPE))
    for j in range(C_HEADS // 2):
        rope = lambda hd: (SRC_Q + hd * (C_NOPE + C_ROPE) + C_NOPE, C_ROPE)
        put(COL_QR + j * 128, rope(2 * j), rope(2 * j + 1))
    for dst, src, width in ((COL_CZ, SRC_CZ, C_WIDTH), (COL_CKV, SRC_CKV, KV_RANK), (COL_AU, SRC_AB, SRC_Q)):
        for c in range(width // 128):
            put(dst + c * 128, (src + c * 128, 128))
    put(COL_KR, (SRC_KR, C_ROPE), (SRC_KR, C_ROPE))
    o_ref[:, IN_USED:IN_PAD] = jnp.zeros((o_ref.shape[0], IN_PAD - IN_USED), BF16)


def _permute_w_in(w):
    depth = w.shape[0]
    return pl.pallas_call(
        _w_in_kernel,
        grid=(depth, D_MODEL // TM_WPREP),
        in_specs=[pl.BlockSpec((None, IN_WIDTH, TM_WPREP), lambda l, i: (l, 0, i))],
        out_specs=pl.BlockSpec((None, TM_WPREP, IN_PAD), lambda l, i: (l, i, 0)),
        out_shape=jax.ShapeDtypeStruct((depth, D_MODEL, IN_PAD), BF16),
        compiler_params=_params(("arbitrary", "arbitrary")),
        name="w_in_prep",
    )(jnp.swapaxes(w, 1, 2))


def _score_shift(gqn, gqr, gkn, gkr):
    amax = lambda g: jnp.max(jnp.abs(g), axis=-1)
    bound = Q_SCALE * (C_NOPE * amax(gqn) * amax(gkn) + C_ROPE * amax(gqr) * amax(gkr))
    shift = jnp.ceil(bound * 1.02)
    bounded = shift <= MAX_SHIFT
    return jnp.where(bounded, shift, 0.0), bounded


def kernel(x, p, positions, attn_norm, w_in, sgu_norm, w_spatial, b_spatial, conv_w, conv_b, kv_norm, w_ukv,
           q_nope_norm, q_rope_norm, k_nope_norm, k_rope_norm, out_norm, w_out, ple_norm, w_ple_gate, w_ple_proj):
    bsz, s_len, d = x.shape
    depth = w_in.shape[0]
    m = bsz * s_len

    inv = 1.0 / (ROPE_BASE ** (jnp.arange(0, C_ROPE, 2, dtype=F32) / C_ROPE))
    ang = positions.astype(F32).reshape(m, 1) * jnp.tile(inv, 4)[None, :]
    sign = jnp.tile(jnp.concatenate([-jnp.ones(C_ROPE // 2, F32), jnp.ones(C_ROPE // 2, F32)]), 2)
    cos128 = jnp.cos(ang)
    sin128 = jnp.sin(ang) * sign[None, :]

    vec = lambda g: g[:, None, :]
    w_in_p = _permute_w_in(w_in)
    w_ukv_b = w_ukv.astype(BF16)
    shift, bounded = _score_shift(q_nope_norm, q_rope_norm, k_nope_norm, k_rope_norm)
    kpad = jnp.zeros((depth, 1, 128), F32).at[:, 0, C_ROPE].set(-shift)
    gqr2 = vec(jnp.tile(q_rope_norm, (1, 2)))
    gkr2 = vec(jnp.tile(k_rope_norm, (1, 2)))
    bs_full = jnp.broadcast_to(b_spatial[..., None], (depth, A_HEADS, CHUNK, A_HEAD))
    ws_b, wout_b = w_spatial.astype(BF16), w_out.astype(BF16)
    wgate_b, wpp_b = w_ple_gate.astype(BF16), w_ple_proj.astype(BF16)
    p2 = p.reshape(depth, m, PLE_DIM)

    h = x.reshape(m, d)
    for i in range(depth):
        lid = jnp.full((1,), i, jnp.int32)
        proj, q, k, v = _in_proj(lid, h, vec(attn_norm), w_in_p, cos128, sin128, vec(q_nope_norm), gqr2,
                                 vec(kv_norm), w_ukv_b, vec(k_nope_norm), gkr2, kpad, bsz, s_len)
        o = _flash(q, k, v, bounded[i])
        h = _mix_out(lid, proj, o, h, p2, sgu_norm, ws_b, bs_full, conv_w, vec(conv_b), vec(out_norm),
                     wout_b, vec(ple_norm), wgate_b, wpp_b, s_len)
    return h.reshape(bsz, s_len, d)
```

```python
import functools
import math

import jax
import jax.numpy as jnp
from jax import lax
from jax.experimental import pallas as pl
from jax.experimental.pallas import tpu as pltpu

F32 = jnp.float32
BF16 = jnp.bfloat16

D_MODEL = 2048
PLE_DIM = 256
EPS = 1e-6
CHUNK = 128
A_HEAD = 128
A_WIDTH = 512
A_HEADS = 4
B_WIDTH = 512
CONV_W = 3
C_WIDTH = 1024
C_V = 128
C_HEADS = 8
C_NOPE = 128
C_ROPE = 64
KV_RANK = 512
ROPE_BASE = 10000.0
QK_PAD = 256
V_PAD = 256
MAX_SHIFT = 50.0

COL_AU, COL_AV, COL_AZ = 0, 512, 1024
COL_BB, COL_BC, COL_BH, COL_BZ = 1536, 2048, 2560, 3072
COL_CZ = 3584
MIX_COLS = 4608
GATE_COLS = ((COL_AZ, 512), (COL_BZ, 512), (COL_CZ, 1024))
SEG = 256
COL_QN = 4608
COL_QR = 5632
COL_CKV = 6144
COL_KR = 6656
IN_USED = 6784
IN_PAD = 6912
PREP_COLS = IN_PAD - MIX_COLS

Q_SCALE = (C_NOPE + C_ROPE) ** -0.5 * math.log2(math.e)

VMEM_LIMIT = 56 * 1024 * 1024

TM_IN, TN_IN = 256, 768
VMEM_LIMIT_IN = 60 * 1024 * 1024
TM_WPREP = 256
TQ, TK = 2048, 256
FLASH_CHAINS = 2
TM_MIX = 512
HALO = 16


def _rms(x, gain):
    ms = jnp.mean(x * x, axis=-1, keepdims=True)
    return x * lax.rsqrt(ms + EPS) * gain


def _silu(z):
    return z * jax.nn.sigmoid(z)


def _layer_spec(shape, buffers=None):
    mode = {} if buffers is None else {"pipeline_mode": pl.Buffered(buffers)}
    return pl.BlockSpec((None,) + shape, lambda *a: (a[-1][0],) + (0,) * len(shape), **mode)


def _params(semantics):
    return pltpu.CompilerParams(dimension_semantics=semantics, vmem_limit_bytes=VMEM_LIMIT)


def _in_proj_kernel(lid_ref, x_ref, g_ref, w_ref, cos_ref, sin_ref, gqn_ref, gqr_ref, gkv_ref,
                    wukv_ref, gkn_ref, gkr_ref, kpad_ref, gsgu_ref, ws_ref, bs_ref, gout_ref,
                    o_ref, q_out, k_out, v_out, hn_ref, ya_ref):
    tm = x_ref.shape[0]
    hn_ref[...] = _rms(x_ref[...], g_ref[...]).astype(BF16)

    def project(col):
        return jnp.dot(hn_ref[...], w_ref[:, col:col + TN_IN], preferred_element_type=F32)

    bc_seen = {}
    a_seen = {}

    def mix_chunk(j):
        lo = j * TN_IN
        res = project(lo)
        for a in range(lo, lo + TN_IN, SEG):
            val = res[:, a - lo:a - lo + SEG]
            if any(start <= a < start + width for start, width in GATE_COLS):
                val = _silu(val)
            if a < COL_BB:
                a_seen[a] = val
            elif COL_BC <= a < COL_BC + B_WIDTH:
                bc_seen[a - COL_BC] = val
            elif COL_BH <= a < COL_BH + B_WIDTH:
                val = val * bc_seen[a - COL_BH]
            o_ref[:, a:a + SEG] = val.astype(o_ref.dtype)

    n_mix = MIX_COLS // TN_IN
    attn = [project(MIX_COLS + k * TN_IN) for k in range(PREP_COLS // TN_IN)]

    def piece(col, width):
        k, off = divmod(col - MIX_COLS, TN_IN)
        assert off + width <= TN_IN
        return attn[k][:, off:off + width]

    cos = cos_ref[...]
    sin = sin_ref[...]
    lane = lax.broadcasted_iota(jnp.int32, (tm, 128), 1)
    low_half = (lane & 32) == 0
    first = lane < 64

    def rope(x):
        swapped = jnp.where(low_half, pltpu.roll(x, 96, 1), pltpu.roll(x, 32, 1))
        return x * cos + swapped * sin

    cn = _rms(piece(COL_CKV, KV_RANK), gkv_ref[...]).astype(BF16)
    mix_chunk(0)
    mix_chunk(1)
    kv = jnp.dot(cn, wukv_ref[...], preferred_element_type=F32)
    def spatial_gating():
        def a_piece(col, rows, hd):
            seg, off = divmod(hd * A_HEAD, SEG)
            return a_seen[col + seg * SEG][rows, off:off + A_HEAD]

        for c in range(tm // CHUNK):
            rows = slice(c * CHUNK, (c + 1) * CHUNK)
            for hd in range(A_HEADS):
                vn = _rms(a_piece(COL_AV, rows, hd), gsgu_ref[hd:hd + 1, :]).astype(BF16)
                s = jnp.dot(ws_ref[hd], vn, preferred_element_type=F32) + bs_ref[hd]
                ya_ref[rows, hd * A_HEAD:(hd + 1) * A_HEAD] = a_piece(COL_AU, rows, hd) * s * a_piece(COL_AZ, rows, hd)
        o_ref[:, COL_AU:COL_AU + A_WIDTH] = _rms(ya_ref[...], gout_ref[:, 0:A_WIDTH]).astype(o_ref.dtype)

    for j in range(2, n_mix):
        mix_chunk(j)
        if j == n_mix - 3:
            spatial_gating()

    for hd in range(C_HEADS):
        x = piece(COL_QN + hd * C_NOPE, C_NOPE)
        q_out[hd, :, 0:C_NOPE] = (_rms(x, gqn_ref[...]) * Q_SCALE).astype(BF16)

    q_pad = jnp.where(lane == C_ROPE, 1.0, 0.0)
    for j in range(C_HEADS // 2):
        x = piece(COL_QR + j * 128, 128)
        x2 = x * x
        s_lo = jnp.sum(jnp.where(first, x2, 0.0), axis=-1, keepdims=True)
        s_hi = jnp.sum(jnp.where(first, 0.0, x2), axis=-1, keepdims=True)
        ms = jnp.where(first, s_lo, s_hi) * (1.0 / C_ROPE)
        r = rope(x * lax.rsqrt(ms + EPS) * gqr_ref[...]) * Q_SCALE
        q_out[2 * j, :, C_NOPE:QK_PAD] = jnp.where(first, r, q_pad).astype(BF16)
        q_out[2 * j + 1, :, C_NOPE:QK_PAD] = jnp.where(first, pltpu.roll(r, 64, 1), q_pad).astype(BF16)

    kr = rope(_rms(piece(COL_KR, 128), gkr_ref[...]))
    kr = jnp.where(first, kr, kpad_ref[...]).astype(BF16)
    ones_col = jnp.where(lane == 0, 1.0, 0.0).astype(BF16)
    for hd in range(C_HEADS):
        base = hd * (C_NOPE + C_V)
        kn = kv[:, base:base + C_NOPE]
        k_out[hd, :, 0:C_NOPE] = _rms(kn, gkn_ref[...]).astype(BF16)
        k_out[hd, :, C_NOPE:QK_PAD] = kr
        v_out[hd, :, 0:C_V] = kv[:, base + C_NOPE:base + C_NOPE + C_V].astype(BF16)
        v_out[hd, :, C_V:V_PAD] = ones_col


def _in_proj(lid, h, gain, w, cos128, sin128, gqn, gqr, gkv, wukv, gkn, gkr, kpad, gsgu, ws, bs, gout,
             bsz, s_len):
    m = h.shape[0]
    tm = TM_IN
    nst = s_len // tm
    rows = lambda width: pl.BlockSpec((tm, width), lambda i, l: (i, 0))
    head_out = lambda width: pl.BlockSpec((None, C_HEADS, tm, width), lambda i, l: (i // nst, 0, i % nst, 0))
    head_shape = lambda width: jax.ShapeDtypeStruct((bsz, C_HEADS, s_len, width), BF16)
    return pl.pallas_call(
        _in_proj_kernel,
        grid_spec=pltpu.PrefetchScalarGridSpec(
            num_scalar_prefetch=1,
            grid=(m // tm,),
            in_specs=[
                rows(D_MODEL), _layer_spec((1, D_MODEL)), _layer_spec((D_MODEL, IN_PAD), 1),
                rows(128), rows(128),
                _layer_spec((1, C_NOPE)), _layer_spec((1, 128)), _layer_spec((1, KV_RANK)),
                _layer_spec((KV_RANK, 2 * C_WIDTH), 1), _layer_spec((1, C_NOPE)), _layer_spec((1, 128)),
                _layer_spec((1, 128)),
                _layer_spec((A_HEADS, A_HEAD)), _layer_spec((A_HEADS, CHUNK, CHUNK)),
                _layer_spec((A_HEADS, CHUNK, A_HEAD)), _layer_spec((1, D_MODEL)),
            ],
            out_specs=[rows(MIX_COLS), head_out(QK_PAD), head_out(QK_PAD), head_out(V_PAD)],
            scratch_shapes=[pltpu.VMEM((tm, D_MODEL), BF16), pltpu.VMEM((tm, A_WIDTH), F32)],
        ),
        out_shape=[jax.ShapeDtypeStruct((m, MIX_COLS), BF16),
                   head_shape(QK_PAD), head_shape(QK_PAD), head_shape(V_PAD)],
        compiler_params=pltpu.CompilerParams(dimension_semantics=("arbitrary",), vmem_limit_bytes=VMEM_LIMIT_IN),
        name="in_proj",
    )(lid, h, gain, w, cos128, sin128, gqn, gqr, gkv, wukv, gkn, gkr, kpad, gsgu, ws, bs, gout)


_NT = (((1,), (1,)), ((), ()))


def _flash_bounded_kernel(q_ref, k_ref, v_ref, o_ref):
    rows = q_ref.shape[0] // FLASH_CHAINS
    s_len = k_ref.shape[0]
    nk = s_len // TK

    def probs(c, j):
        q = q_ref[c * rows:(c + 1) * rows, :]
        s = lax.dot_general(q, k_ref[j * TK:(j + 1) * TK, :], _NT, preferred_element_type=F32)
        return jnp.exp2(s).astype(BF16)

    p = [None] * FLASH_CHAINS
    acc = [jnp.zeros((rows, V_PAD), F32)] * FLASH_CHAINS
    for t in range(-1, nk + FLASH_CHAINS - 1):
        for c in range(FLASH_CHAINS):
            j = t - c
            p_next = probs(c, j + 1) if 0 <= j + 1 < nk else None
            if 0 <= j < nk:
                acc[c] = acc[c] + jnp.dot(p[c], v_ref[j * TK:(j + 1) * TK, :], preferred_element_type=F32)
            if p_next is not None:
                p[c] = p_next
    for c in range(FLASH_CHAINS):
        o_ref[c * rows:(c + 1) * rows, :] = (acc[c][:, 0:C_V] / acc[c][:, C_V:C_V + 1]).astype(o_ref.dtype)


def _flash_online_kernel(q_ref, k_ref, v_ref, o_ref):
    tq = q_ref.shape[0]
    s_len = k_ref.shape[0]
    q = q_ref[...]

    def body(j, carry):
        m, acc = carry
        start = pl.multiple_of(j * TK, TK)
        s = lax.dot_general(q, k_ref[pl.ds(start, TK), :], _NT, preferred_element_type=F32)
        m_new = jnp.maximum(m, jnp.max(s, axis=-1, keepdims=True))
        p = jnp.exp2(s - m_new).astype(BF16)
        acc = jnp.exp2(m - m_new) * acc + jnp.dot(p, v_ref[pl.ds(start, TK), :], preferred_element_type=F32)
        return m_new, acc

    init = (jnp.full((tq, 1), -jnp.inf, F32), jnp.zeros((tq, V_PAD), F32))
    _, acc = lax.fori_loop(0, s_len // TK, body, init)
    o_ref[...] = (acc[:, 0:C_V] / acc[:, C_V:C_V + 1]).astype(o_ref.dtype)


def _flash(q, k, v, bounded):
    bsz, heads, s_len, _ = q.shape
    nq = s_len // TQ

    def call(body, name):
        return pl.pallas_call(
            body,
            grid=(bsz, heads, nq),
            in_specs=[
                pl.BlockSpec((None, None, TQ, QK_PAD), lambda b, hd, i: (b, hd, i, 0)),
                pl.BlockSpec((None, None, s_len, QK_PAD), lambda b, hd, i: (b, hd, 0, 0)),
                pl.BlockSpec((None, None, s_len, V_PAD), lambda b, hd, i: (b, hd, 0, 0)),
            ],
            out_specs=pl.BlockSpec((TQ, C_V), lambda b, hd, i: (b * nq + i, hd)),
            out_shape=jax.ShapeDtypeStruct((bsz * s_len, C_WIDTH), BF16),
            compiler_params=_params(("arbitrary", "arbitrary", "arbitrary")),
            name=name,
        )(q, k, v)

    return lax.cond(bounded,
                    lambda: call(_flash_bounded_kernel, "flash_bounded"),
                    lambda: call(_flash_online_kernel, "flash_online"))


def _mix_kernel(lid_ref, x_ref, xp_ref, xn_ref, o_ref, h_ref, p_ref,
                cw_ref, cb_ref, gout_ref, wout_ref, gple_ref, wgate_ref, wpp_ref,
                out_ref, y_ref, *, s_len):
    tm = h_ref.shape[0]
    i = pl.program_id(0)
    piece = lambda col, width: x_ref[:, col:col + width].astype(F32)

    y_ref[:, 0:A_WIDTH] = x_ref[:, COL_AU:COL_AU + A_WIDTH]

    xc = piece(COL_BH, B_WIDTH)
    row = lax.broadcasted_iota(jnp.int32, xc.shape, 0)
    has_prev = ((i * tm) % s_len != 0).astype(F32)
    has_next = (((i + 1) * tm) % s_len != 0).astype(F32)
    prev_row = xp_ref[HALO - 1:HALO, :].astype(F32) * has_prev
    next_row = xn_ref[0:1, :].astype(F32) * has_next
    x_prev = jnp.where(row == 0, prev_row, pltpu.roll(xc, 1, 0))
    x_next = jnp.where(row == tm - 1, next_row, pltpu.roll(xc, tm - 1, 0))
    conv = cb_ref[...] + x_prev * cw_ref[0:1, :] + xc * cw_ref[1:2, :] + x_next * cw_ref[2:3, :]
    yb = piece(COL_BB, B_WIDTH) * conv * piece(COL_BZ, B_WIDTH)
    y_ref[:, A_WIDTH:A_WIDTH + B_WIDTH] = _rms(yb, gout_ref[:, A_WIDTH:A_WIDTH + B_WIDTH]).astype(BF16)

    yc = o_ref[...].astype(F32) * piece(COL_CZ, C_WIDTH)
    y_ref[:, A_WIDTH + B_WIDTH:] = _rms(yc, gout_ref[:, A_WIDTH + B_WIDTH:]).astype(BF16)

    hm = h_ref[...] + jnp.dot(y_ref[...], wout_ref[...], preferred_element_type=F32)
    hn = _rms(hm, gple_ref[...]).astype(BF16)
    gate = jax.nn.sigmoid(jnp.dot(hn, wgate_ref[...], preferred_element_type=F32))
    pp = jnp.dot(p_ref[...].astype(BF16), wpp_ref[...], preferred_element_type=F32)
    out_ref[...] = hm + gate * pp


def _mix_out(lid, proj, o, h, p, cw, cb, gout, wout, gple, wgate, wpp, s_len):
    m = h.shape[0]
    tm = TM_MIX
    nhalo = tm // HALO
    last_halo = m // HALO - 1
    halo_col = COL_BH // B_WIDTH
    return pl.pallas_call(
        functools.partial(_mix_kernel, s_len=s_len),
        grid_spec=pltpu.PrefetchScalarGridSpec(
            num_scalar_prefetch=1,
            grid=(m // tm,),
            in_specs=[
                pl.BlockSpec((tm, MIX_COLS), lambda i, l: (i, 0)),
                pl.BlockSpec((HALO, B_WIDTH), lambda i, l: (jnp.maximum(i * nhalo - 1, 0), halo_col)),
                pl.BlockSpec((HALO, B_WIDTH), lambda i, l: (jnp.minimum((i + 1) * nhalo, last_halo), halo_col)),
                pl.BlockSpec((tm, C_WIDTH), lambda i, l: (i, 0)),
                pl.BlockSpec((tm, D_MODEL), lambda i, l: (i, 0)),
                pl.BlockSpec((None, tm, PLE_DIM), lambda i, l: (l[0], i, 0)),
                _layer_spec((CONV_W, B_WIDTH)), _layer_spec((1, B_WIDTH)), _layer_spec((1, D_MODEL)),
                _layer_spec((D_MODEL, D_MODEL), 1), _layer_spec((1, D_MODEL)), _layer_spec((D_MODEL, D_MODEL), 1),
                _layer_spec((PLE_DIM, D_MODEL), 1),
            ],
            out_specs=pl.BlockSpec((tm, D_MODEL), lambda i, l: (i, 0)),
            scratch_shapes=[pltpu.VMEM((tm, D_MODEL), BF16)],
        ),
        out_shape=jax.ShapeDtypeStruct((m, D_MODEL), F32),
        compiler_params=pltpu.CompilerParams(dimension_semantics=("arbitrary",), vmem_limit_bytes=VMEM_LIMIT_IN),
        name="mix_out",
    )(lid, proj, proj, proj, o, h, p, cw, cb, gout, wout, gple, wgate, wpp)


SRC_AB = 0
SRC_Q = 3 * A_WIDTH + 4 * B_WIDTH
SRC_CKV = SRC_Q + C_HEADS * (C_NOPE + C_ROPE)
SRC_KR = SRC_CKV + KV_RANK
SRC_CZ = SRC_KR + C_ROPE
IN_WIDTH = SRC_CZ + C_WIDTH


def _w_in_kernel(wt_ref, o_ref):
    def put(dst, *pieces):
        rows = [wt_ref[src:src + n, :] for src, n in pieces]
        block = rows[0] if len(rows) == 1 else jnp.concatenate(rows, axis=0)
        o_ref[:, dst:dst + 128] = block.T.astype(BF16)

    for hd in range(C_HEADS):
        put(COL_QN + hd * C_NOPE, (SRC_Q + hd * (C_NOPE + C_ROPE), C_NOPE))
    for j in range(C_HEADS // 2):
        rope = lambda hd: (SRC_Q + hd * (C_NOPE + C_ROPE) + C_NOPE, C_ROPE)
        put(COL_QR + j * 128, rope(2 * j), rope(2 * j + 1))
    for dst, src, width in ((COL_CZ, SRC_CZ, C_WIDTH), (COL_CKV, SRC_CKV, KV_RANK), (COL_AU, SRC_AB, SRC_Q)):
        for c in range(width // 128):
            put(dst + c * 128, (src + c * 128, 128))
    put(COL_KR, (SRC_KR, C_ROPE), (SRC_KR, C_ROPE))
    o_ref[:, IN_USED:IN_PAD] = jnp.zeros((o_ref.shape[0], IN_PAD - IN_USED), BF16)


def _permute_w_in(w):
    depth = w.shape[0]
    return pl.pallas_call(
        _w_in_kernel,
        grid=(depth, D_MODEL // TM_WPREP),
        in_specs=[pl.BlockSpec((None, IN_WIDTH, TM_WPREP), lambda l, i: (l, 0, i))],
        out_specs=pl.BlockSpec((None, TM_WPREP, IN_PAD), lambda l, i: (l, i, 0)),
        out_shape=jax.ShapeDtypeStruct((depth, D_MODEL, IN_PAD), BF16),
        compiler_params=_params(("arbitrary", "arbitrary")),
        name="w_in_prep",
    )(jnp.swapaxes(w, 1, 2))


def _score_shift(gqn, gqr, gkn, gkr):
    amax = lambda g: jnp.max(jnp.abs(g), axis=-1)
    bound = Q_SCALE * (C_NOPE * amax(gqn) * amax(gkn) + C_ROPE * amax(gqr) * amax(gkr))
    shift = jnp.ceil(bound * 1.02)
    bounded = shift <= MAX_SHIFT
    return jnp.where(bounded, shift, 0.0), bounded


def kernel(x, p, positions, attn_norm, w_in, sgu_norm, w_spatial, b_spatial, conv_w, conv_b, kv_norm, w_ukv,
           q_nope_norm, q_rope_norm, k_nope_norm, k_rope_norm, out_norm, w_out, ple_norm, w_ple_gate, w_ple_proj):
    bsz, s_len, d = x.shape
    depth = w_in.shape[0]
    m = bsz * s_len

    inv = 1.0 / (ROPE_BASE ** (jnp.arange(0, C_ROPE, 2, dtype=F32) / C_ROPE))
    ang = positions.astype(F32).reshape(m, 1) * jnp.tile(inv, 4)[None, :]
    sign = jnp.tile(jnp.concatenate([-jnp.ones(C_ROPE // 2, F32), jnp.ones(C_ROPE // 2, F32)]), 2)
    cos128 = jnp.cos(ang)
    sin128 = jnp.sin(ang) * sign[None, :]

    vec = lambda g: g[:, None, :]
    w_in_p = _permute_w_in(w_in)
    w_ukv_b = w_ukv.astype(BF16)
    shift, bounded = _score_shift(q_nope_norm, q_rope_norm, k_nope_norm, k_rope_norm)
    kpad = jnp.zeros((depth, 1, 128), F32).at[:, 0, C_ROPE].set(-shift)
    gqr2 = vec(jnp.tile(q_rope_norm, (1, 2)))
    gkr2 = vec(jnp.tile(k_rope_norm, (1, 2)))
    bs_full = jnp.broadcast_to(b_spatial[..., None], (depth, A_HEADS, CHUNK, A_HEAD))
    ws_b, wout_b = w_spatial.astype(BF16), w_out.astype(BF16)
    wgate_b, wpp_b = w_ple_gate.astype(BF16), w_ple_proj.astype(BF16)
    p2 = p.reshape(depth, m, PLE_DIM)

    h = x.reshape(m, d)
    for i in range(depth):
        lid = jnp.full((1,), i, jnp.int32)
        proj, q, k, v = _in_proj(lid, h, vec(attn_norm), w_in_p, cos128, sin128, vec(q_nope_norm), gqr2,
                                 vec(kv_norm), w_ukv_b, vec(k_nope_norm), gkr2, kpad,
                                 sgu_norm, ws_b, bs_full, vec(out_norm), bsz, s_len)
        o = _flash(q, k, v, bounded[i])
        h = _mix_out(lid, proj, o, h, p2, conv_w, vec(conv_b), vec(out_norm), wout_b, vec(ple_norm),
                     wgate_b, wpp_b, s_len)
    return h.reshape(bsz, s_len, d)
```

```python
import functools
import math

import jax
import jax.numpy as jnp
from jax import lax
from jax.experimental import pallas as pl
from jax.experimental.pallas import tpu as pltpu

F32 = jnp.float32
BF16 = jnp.bfloat16

D_MODEL = 2048
PLE_DIM = 256
EPS = 1e-6
CHUNK = 128
A_HEAD = 128
A_WIDTH = 512
A_HEADS = 4
B_WIDTH = 512
CONV_W = 3
C_WIDTH = 1024
C_V = 128
C_HEADS = 8
C_NOPE = 128
C_ROPE = 64
KV_RANK = 512
ROPE_BASE = 10000.0
QK_PAD = 256
V_PAD = 256
MAX_SHIFT = 50.0

COL_AU, COL_AV, COL_AZ = 0, 512, 1024
COL_BB, COL_BC, COL_BH, COL_BZ = 1536, 2048, 2560, 3072
COL_CZ = 3584
MIX_COLS = 4608
GATE_COLS = ((COL_AZ, A_WIDTH), (COL_BZ, B_WIDTH), (COL_CZ, C_WIDTH))
SEG = 256
COL_QN = 4608
COL_QR = 5632
COL_CKV = 6144
COL_KR = 6656
IN_USED = 6784
IN_PAD = 6912
PREP_COLS = IN_PAD - MIX_COLS

Q_SCALE = (C_NOPE + C_ROPE) ** -0.5 * math.log2(math.e)

VMEM_LIMIT = 56 * 1024 * 1024

TM_IN, TN_IN = 256, 768
VMEM_LIMIT_IN = 60 * 1024 * 1024
TM_WPREP = 256
TQ, TK = 2048, 256
FLASH_CHAINS = 2
TM_MIX = 512
HALO = 16


def _rms(x, gain):
    ms = jnp.mean(x * x, axis=-1, keepdims=True)
    return x * lax.rsqrt(ms + EPS) * gain


def _silu(z):
    return z * jax.nn.sigmoid(z)


def _layer_spec(shape, buffers=None):
    mode = {} if buffers is None else {"pipeline_mode": pl.Buffered(buffers)}
    return pl.BlockSpec((None,) + shape, lambda *a: (a[-1][0],) + (0,) * len(shape), **mode)


def _params(semantics):
    return pltpu.CompilerParams(dimension_semantics=semantics, vmem_limit_bytes=VMEM_LIMIT)


def _in_proj_kernel(lid_ref, x_ref, g_ref, w_ref, cos_ref, sin_ref, gqn_ref, gqr_ref, gkv_ref,
                    wukv_ref, gkn_ref, gkr_ref, kpad_ref, gsgu_ref, ws_ref, bs_ref, gout_ref,
                    o_ref, q_out, k_out, v_out, hn_ref, ya_ref):
    tm = x_ref.shape[0]
    hn_ref[...] = _rms(x_ref[...], g_ref[...]).astype(BF16)

    def project(col):
        return jnp.dot(hn_ref[...], w_ref[:, col:col + TN_IN], preferred_element_type=F32)

    bc_seen = {}
    a_seen = {}

    def mix_chunk(j):
        lo = j * TN_IN
        res = project(lo)
        for a in range(lo, lo + TN_IN, SEG):
            val = res[:, a - lo:a - lo + SEG]
            if any(start <= a < start + width for start, width in GATE_COLS):
                val = _silu(val)
            if a < COL_BB:
                a_seen[a] = val
            elif COL_BC <= a < COL_BC + B_WIDTH:
                bc_seen[a - COL_BC] = val
            elif COL_BH <= a < COL_BH + B_WIDTH:
                val = val * bc_seen[a - COL_BH]
            o_ref[:, a:a + SEG] = val.astype(o_ref.dtype)

    n_mix = MIX_COLS // TN_IN
    attn = [project(MIX_COLS + k * TN_IN) for k in range(PREP_COLS // TN_IN)]

    def piece(col, width):
        k, off = divmod(col - MIX_COLS, TN_IN)
        assert off + width <= TN_IN
        return attn[k][:, off:off + width]

    cos = cos_ref[...]
    sin = sin_ref[...]
    lane = lax.broadcasted_iota(jnp.int32, (tm, 128), 1)
    low_half = (lane & 32) == 0
    first = lane < 64

    def rope(x):
        swapped = jnp.where(low_half, pltpu.roll(x, 96, 1), pltpu.roll(x, 32, 1))
        return x * cos + swapped * sin

    cn = _rms(piece(COL_CKV, KV_RANK), gkv_ref[...]).astype(BF16)
    mix_chunk(0)
    mix_chunk(1)
    kv = jnp.dot(cn, wukv_ref[...], preferred_element_type=F32)

    def spatial_gating():
        def a_piece(col, rows, hd):
            seg, off = divmod(hd * A_HEAD, SEG)
            return a_seen[col + seg * SEG][rows, off:off + A_HEAD]

        for c in range(tm // CHUNK):
            rows = slice(c * CHUNK, (c + 1) * CHUNK)
            for hd in range(A_HEADS):
                vn = _rms(a_piece(COL_AV, rows, hd), gsgu_ref[hd:hd + 1, :]).astype(BF16)
                s = jnp.dot(ws_ref[hd], vn, preferred_element_type=F32) + bs_ref[hd]
                gated = a_piece(COL_AU, rows, hd) * s * a_piece(COL_AZ, rows, hd)
                ya_ref[rows, hd * A_HEAD:(hd + 1) * A_HEAD] = gated
        o_ref[:, COL_AU:COL_AU + A_WIDTH] = _rms(ya_ref[...], gout_ref[:, 0:A_WIDTH]).astype(o_ref.dtype)

    for j in range(2, n_mix):
        mix_chunk(j)
        if j == n_mix - 3:
            spatial_gating()

    for hd in range(C_HEADS):
        x = piece(COL_QN + hd * C_NOPE, C_NOPE)
        q_out[hd, :, 0:C_NOPE] = (_rms(x, gqn_ref[...]) * Q_SCALE).astype(BF16)

    q_pad = jnp.where(lane == C_ROPE, 1.0, 0.0)
    for j in range(C_HEADS // 2):
        x = piece(COL_QR + j * 128, 128)
        x2 = x * x
        s_lo = jnp.sum(jnp.where(first, x2, 0.0), axis=-1, keepdims=True)
        s_hi = jnp.sum(jnp.where(first, 0.0, x2), axis=-1, keepdims=True)
        ms = jnp.where(first, s_lo, s_hi) * (1.0 / C_ROPE)
        r = rope(x * lax.rsqrt(ms + EPS) * gqr_ref[...]) * Q_SCALE
        q_out[2 * j, :, C_NOPE:QK_PAD] = jnp.where(first, r, q_pad).astype(BF16)
        q_out[2 * j + 1, :, C_NOPE:QK_PAD] = jnp.where(first, pltpu.roll(r, 64, 1), q_pad).astype(BF16)

    kr = rope(_rms(piece(COL_KR, 128), gkr_ref[...]))
    kr = jnp.where(first, kr, kpad_ref[...]).astype(BF16)
    ones_col = jnp.where(lane == 0, 1.0, 0.0).astype(BF16)
    for hd in range(C_HEADS):
        base = hd * (C_NOPE + C_V)
        kn = kv[:, base:base + C_NOPE]
        k_out[hd, :, 0:C_NOPE] = _rms(kn, gkn_ref[...]).astype(BF16)
        k_out[hd, :, C_NOPE:QK_PAD] = kr
        v_out[hd, :, 0:C_V] = kv[:, base + C_NOPE:base + C_NOPE + C_V].astype(BF16)
        v_out[hd, :, C_V:V_PAD] = ones_col


def _in_proj(lid, h, gain, w, cos128, sin128, gqn, gqr, gkv, wukv, gkn, gkr, kpad, gsgu, ws, bs, gout,
             bsz, s_len):
    m = h.shape[0]
    tm = TM_IN
    nst = s_len // tm
    rows = lambda width: pl.BlockSpec((tm, width), lambda i, l: (i, 0))
    head_out = lambda width: pl.BlockSpec((None, C_HEADS, tm, width), lambda i, l: (i // nst, 0, i % nst, 0))
    head_shape = lambda width: jax.ShapeDtypeStruct((bsz, C_HEADS, s_len, width), BF16)
    return pl.pallas_call(
        _in_proj_kernel,
        grid_spec=pltpu.PrefetchScalarGridSpec(
            num_scalar_prefetch=1,
            grid=(m // tm,),
            in_specs=[
                rows(D_MODEL), _layer_spec((1, D_MODEL)), _layer_spec((D_MODEL, IN_PAD), 1),
                rows(128), rows(128),
                _layer_spec((1, C_NOPE)), _layer_spec((1, 128)), _layer_spec((1, KV_RANK)),
                _layer_spec((KV_RANK, 2 * C_WIDTH), 1), _layer_spec((1, C_NOPE)), _layer_spec((1, 128)),
                _layer_spec((1, 128)),
                _layer_spec((A_HEADS, A_HEAD)), _layer_spec((A_HEADS, CHUNK, CHUNK)),
                _layer_spec((A_HEADS, CHUNK, A_HEAD)), _layer_spec((1, D_MODEL)),
            ],
            out_specs=[rows(MIX_COLS), head_out(QK_PAD), head_out(QK_PAD), head_out(V_PAD)],
            scratch_shapes=[pltpu.VMEM((tm, D_MODEL), BF16), pltpu.VMEM((tm, A_WIDTH), F32)],
        ),
        out_shape=[jax.ShapeDtypeStruct((m, MIX_COLS), BF16),
                   head_shape(QK_PAD), head_shape(QK_PAD), head_shape(V_PAD)],
        compiler_params=pltpu.CompilerParams(dimension_semantics=("arbitrary",), vmem_limit_bytes=VMEM_LIMIT_IN),
        name="in_proj",
    )(lid, h, gain, w, cos128, sin128, gqn, gqr, gkv, wukv, gkn, gkr, kpad, gsgu, ws, bs, gout)


_NT = (((1,), (1,)), ((), ()))


def _flash_bounded_kernel(q_ref, k_ref, v_ref, o_ref):
    rows = q_ref.shape[0] // FLASH_CHAINS
    s_len = k_ref.shape[0]
    nk = s_len // TK

    def probs(c, j):
        q = q_ref[c * rows:(c + 1) * rows, :]
        s = lax.dot_general(q, k_ref[j * TK:(j + 1) * TK, :], _NT, preferred_element_type=F32)
        return jnp.exp2(s).astype(BF16)

    p = [None] * FLASH_CHAINS
    acc = [jnp.zeros((rows, V_PAD), F32)] * FLASH_CHAINS
    for t in range(-1, nk + FLASH_CHAINS - 1):
        for c in range(FLASH_CHAINS):
            j = t - c
            p_next = probs(c, j + 1) if 0 <= j + 1 < nk else None
            if 0 <= j < nk:
                acc[c] = acc[c] + jnp.dot(p[c], v_ref[j * TK:(j + 1) * TK, :], preferred_element_type=F32)
            if p_next is not None:
                p[c] = p_next
    for c in range(FLASH_CHAINS):
        o_ref[c * rows:(c + 1) * rows, :] = (acc[c][:, 0:C_V] / acc[c][:, C_V:C_V + 1]).astype(o_ref.dtype)


def _flash_online_kernel(q_ref, k_ref, v_ref, o_ref):
    tq = q_ref.shape[0]
    s_len = k_ref.shape[0]
    q = q_ref[...]

    def body(j, carry):
        m, acc = carry
        start = pl.multiple_of(j * TK, TK)
        s = lax.dot_general(q, k_ref[pl.ds(start, TK), :], _NT, preferred_element_type=F32)
        m_new = jnp.maximum(m, jnp.max(s, axis=-1, keepdims=True))
        p = jnp.exp2(s - m_new).astype(BF16)
        acc = jnp.exp2(m - m_new) * acc + jnp.dot(p, v_ref[pl.ds(start, TK), :], preferred_element_type=F32)
        return m_new, acc

    init = (jnp.full((tq, 1), -jnp.inf, F32), jnp.zeros((tq, V_PAD), F32))
    _, acc = lax.fori_loop(0, s_len // TK, body, init)
    o_ref[...] = (acc[:, 0:C_V] / acc[:, C_V:C_V + 1]).astype(o_ref.dtype)


def _flash(q, k, v, bounded):
    bsz, heads, s_len, _ = q.shape
    nq = s_len // TQ

    def call(body, name):
        return pl.pallas_call(
            body,
            grid=(bsz, heads, nq),
            in_specs=[
                pl.BlockSpec((None, None, TQ, QK_PAD), lambda b, hd, i: (b, hd, i, 0)),
                pl.BlockSpec((None, None, s_len, QK_PAD), lambda b, hd, i: (b, hd, 0, 0)),
                pl.BlockSpec((None, None, s_len, V_PAD), lambda b, hd, i: (b, hd, 0, 0)),
            ],
            out_specs=pl.BlockSpec((TQ, C_V), lambda b, hd, i: (b * nq + i, hd)),
            out_shape=jax.ShapeDtypeStruct((bsz * s_len, C_WIDTH), BF16),
            compiler_params=_params(("arbitrary", "arbitrary", "arbitrary")),
            name=name,
        )(q, k, v)

    return lax.cond(bounded,
                    lambda: call(_flash_bounded_kernel, "flash_bounded"),
                    lambda: call(_flash_online_kernel, "flash_online"))


def _mix_kernel(lid_ref, x_ref, xp_ref, xn_ref, o_ref, h_ref, p_ref,
                cw_ref, cb_ref, gout_ref, wout_ref, gple_ref, wgate_ref, wpp_ref,
                out_ref, y_ref, *, s_len):
    tm = h_ref.shape[0]
    i = pl.program_id(0)
    piece = lambda col, width: x_ref[:, col:col + width].astype(F32)

    y_ref[:, 0:A_WIDTH] = x_ref[:, COL_AU:COL_AU + A_WIDTH]

    xc = piece(COL_BH, B_WIDTH)
    row = lax.broadcasted_iota(jnp.int32, xc.shape, 0)
    has_prev = ((i * tm) % s_len != 0).astype(F32)
    has_next = (((i + 1) * tm) % s_len != 0).astype(F32)
    prev_row = xp_ref[HALO - 1:HALO, :].astype(F32) * has_prev
    next_row = xn_ref[0:1, :].astype(F32) * has_next
    x_prev = jnp.where(row == 0, prev_row, pltpu.roll(xc, 1, 0))
    x_next = jnp.where(row == tm - 1, next_row, pltpu.roll(xc, tm - 1, 0))
    conv = cb_ref[...] + x_prev * cw_ref[0:1, :] + xc * cw_ref[1:2, :] + x_next * cw_ref[2:3, :]
    yb = piece(COL_BB, B_WIDTH) * conv * piece(COL_BZ, B_WIDTH)
    y_ref[:, A_WIDTH:A_WIDTH + B_WIDTH] = _rms(yb, gout_ref[:, A_WIDTH:A_WIDTH + B_WIDTH]).astype(BF16)

    yc = o_ref[...].astype(F32) * piece(COL_CZ, C_WIDTH)
    y_ref[:, A_WIDTH + B_WIDTH:] = _rms(yc, gout_ref[:, A_WIDTH + B_WIDTH:]).astype(BF16)

    hm = h_ref[...] + jnp.dot(y_ref[...], wout_ref[...], preferred_element_type=F32)
    hn = _rms(hm, gple_ref[...]).astype(BF16)
    gate = jax.nn.sigmoid(jnp.dot(hn, wgate_ref[...], preferred_element_type=F32))
    pp = jnp.dot(p_ref[...].astype(BF16), wpp_ref[...], preferred_element_type=F32)
    out_ref[...] = hm + gate * pp


def _mix_out(lid, proj, o, h, p, cw, cb, gout, wout, gple, wgate, wpp, s_len):
    m = h.shape[0]
    tm = TM_MIX
    nhalo = tm // HALO
    last_halo = m // HALO - 1
    halo_col = COL_BH // B_WIDTH
    return pl.pallas_call(
        functools.partial(_mix_kernel, s_len=s_len),
        grid_spec=pltpu.PrefetchScalarGridSpec(
            num_scalar_prefetch=1,
            grid=(m // tm,),
            in_specs=[
                pl.BlockSpec((tm, MIX_COLS), lambda i, l: (i, 0)),
                pl.BlockSpec((HALO, B_WIDTH), lambda i, l: (jnp.maximum(i * nhalo - 1, 0), halo_col)),
                pl.BlockSpec((HALO, B_WIDTH), lambda i, l: (jnp.minimum((i + 1) * nhalo, last_halo), halo_col)),
                pl.BlockSpec((tm, C_WIDTH), lambda i, l: (i, 0)),
                pl.BlockSpec((tm, D_MODEL), lambda i, l: (i, 0)),
                pl.BlockSpec((None, tm, PLE_DIM), lambda i, l: (l[0], i, 0)),
                _layer_spec((CONV_W, B_WIDTH)), _layer_spec((1, B_WIDTH)), _layer_spec((1, D_MODEL)),
                _layer_spec((D_MODEL, D_MODEL), 1), _layer_spec((1, D_MODEL)), _layer_spec((D_MODEL, D_MODEL), 1),
                _layer_spec((PLE_DIM, D_MODEL), 1),
            ],
            out_specs=pl.BlockSpec((tm, D_MODEL), lambda i, l: (i, 0)),
            scratch_shapes=[pltpu.VMEM((tm, D_MODEL), BF16)],
        ),
        out_shape=jax.ShapeDtypeStruct((m, D_MODEL), F32),
        compiler_params=pltpu.CompilerParams(dimension_semantics=("arbitrary",), vmem_limit_bytes=VMEM_LIMIT_IN),
        name="mix_out",
    )(lid, proj, proj, proj, o, h, p, cw, cb, gout, wout, gple, wgate, wpp)


SRC_AB = 0
SRC_Q = 3 * A_WIDTH + 4 * B_WIDTH
SRC_CKV = SRC_Q + C_HEADS * (C_NOPE + C_ROPE)
SRC_KR = SRC_CKV + KV_RANK
SRC_CZ = SRC_KR + C_ROPE
IN_WIDTH = SRC_CZ + C_WIDTH


def _w_in_kernel(wt_ref, o_ref):
    def put(dst, *pieces):
        rows = [wt_ref[src:src + n, :] for src, n in pieces]
        block = rows[0] if len(rows) == 1 else jnp.concatenate(rows, axis=0)
        o_ref[:, dst:dst + 128] = block.T.astype(BF16)

    for hd in range(C_HEADS):
        put(COL_QN + hd * C_NOPE, (SRC_Q + hd * (C_NOPE + C_ROPE), C_NOPE))
    for j in range(C_HEADS // 2):
        rope = lambda hd: (SRC_Q + hd * (C_NOPE + C_ROPE) + C_NOPE, C_ROPE)
        put(COL_QR + j * 128, rope(2 * j), rope(2 * j + 1))
    for dst, src, width in ((COL_CZ, SRC_CZ, C_WIDTH), (COL_CKV, SRC_CKV, KV_RANK), (COL_AU, SRC_AB, SRC_Q)):
        for c in range(width // 128):
            put(dst + c * 128, (src + c * 128, 128))
    put(COL_KR, (SRC_KR, C_ROPE), (SRC_KR, C_ROPE))
    o_ref[:, IN_USED:IN_PAD] = jnp.zeros((o_ref.shape[0], IN_PAD - IN_USED), BF16)


def _permute_w_in(w):
    depth = w.shape[0]
    return pl.pallas_call(
        _w_in_kernel,
        grid=(depth, D_MODEL // TM_WPREP),
        in_specs=[pl.BlockSpec((None, IN_WIDTH, TM_WPREP), lambda l, i: (l, 0, i))],
        out_specs=pl.BlockSpec((None, TM_WPREP, IN_PAD), lambda l, i: (l, i, 0)),
        out_shape=jax.ShapeDtypeStruct((depth, D_MODEL, IN_PAD), BF16),
        compiler_params=_params(("arbitrary", "arbitrary")),
        name="w_in_prep",
    )(jnp.swapaxes(w, 1, 2))


def _score_shift(gqn, gqr, gkn, gkr):
    amax = lambda g: jnp.max(jnp.abs(g), axis=-1)
    bound = Q_SCALE * (C_NOPE * amax(gqn) * amax(gkn) + C_ROPE * amax(gqr) * amax(gkr))
    shift = jnp.ceil(bound * 1.02)
    bounded = shift <= MAX_SHIFT
    return jnp.where(bounded, shift, 0.0), bounded


def kernel(x, p, positions, attn_norm, w_in, sgu_norm, w_spatial, b_spatial, conv_w, conv_b, kv_norm, w_ukv,
           q_nope_norm, q_rope_norm, k_nope_norm, k_rope_norm, out_norm, w_out, ple_norm, w_ple_gate, w_ple_proj):
    bsz, s_len, d = x.shape
    depth = w_in.shape[0]
    m = bsz * s_len

    inv = 1.0 / (ROPE_BASE ** (jnp.arange(0, C_ROPE, 2, dtype=F32) / C_ROPE))
    ang = positions.astype(F32).reshape(m, 1) * jnp.tile(inv, 4)[None, :]
    sign = jnp.tile(jnp.concatenate([-jnp.ones(C_ROPE // 2, F32), jnp.ones(C_ROPE // 2, F32)]), 2)
    cos128 = jnp.cos(ang)
    sin128 = jnp.sin(ang) * sign[None, :]

    vec = lambda g: g[:, None, :]
    w_in_p = _permute_w_in(w_in)
    w_ukv_b = w_ukv.astype(BF16)
    shift, bounded = _score_shift(q_nope_norm, q_rope_norm, k_nope_norm, k_rope_norm)
    kpad = jnp.zeros((depth, 1, 128), F32).at[:, 0, C_ROPE].set(-shift)
    gqr2 = vec(jnp.tile(q_rope_norm, (1, 2)))
    gkr2 = vec(jnp.tile(k_rope_norm, (1, 2)))
    bs_full = jnp.broadcast_to(b_spatial[..., None], (depth, A_HEADS, CHUNK, A_HEAD))
    ws_b, wout_b = w_spatial.astype(BF16), w_out.astype(BF16)
    wgate_b, wpp_b = w_ple_gate.astype(BF16), w_ple_proj.astype(BF16)
    p2 = p.reshape(depth, m, PLE_DIM)

    h = x.reshape(m, d)
    for i in range(depth):
        lid = jnp.full((1,), i, jnp.int32)
        proj, q, k, v = _in_proj(lid, h, vec(attn_norm), w_in_p, cos128, sin128, vec(q_nope_norm), gqr2,
                                 vec(kv_norm), w_ukv_b, vec(k_nope_norm), gkr2, kpad,
                                 sgu_norm, ws_b, bs_full, vec(out_norm), bsz, s_len)
        o = _flash(q, k, v, bounded[i])
        h = _mix_out(lid, proj, o, h, p2, conv_w, vec(conv_b), vec(out_norm), wout_b, vec(ple_norm),
                     wgate_b, wpp_b, s_len)
    return h.reshape(bsz, s_len, d)
```

```python
import functools
import math

import jax
import jax.numpy as jnp
from jax import lax
from jax.experimental import pallas as pl
from jax.experimental.pallas import tpu as pltpu

F32 = jnp.float32
BF16 = jnp.bfloat16

D_MODEL = 2048
PLE_DIM = 256
EPS = 1e-6
CHUNK = 128
A_HEAD = 128
A_WIDTH = 512
A_HEADS = 4
B_WIDTH = 512
CONV_W = 3
C_WIDTH = 1024
C_V = 128
C_HEADS = 8
C_NOPE = 128
C_ROPE = 64
KV_RANK = 512
ROPE_BASE = 10000.0
QK_PAD = 256
V_PAD = 256
MAX_SHIFT = 50.0

COL_AU, COL_AV, COL_AZ = 0, 512, 1024
COL_BB, COL_BC, COL_BH, COL_BZ = 1536, 2048, 2560, 3072
COL_CZ = 3584
MIX_COLS = 4608
GATE_COLS = ((COL_AZ, A_WIDTH), (COL_BZ, B_WIDTH), (COL_CZ, C_WIDTH))
SEG = 256
COL_QN = 4608
COL_QR = 5632
COL_CKV = 6144
COL_KR = 6656
IN_USED = 6784
IN_PAD = 6912
PREP_COLS = IN_PAD - MIX_COLS

Q_SCALE = (C_NOPE + C_ROPE) ** -0.5 * math.log2(math.e)

VMEM_LIMIT = 56 * 1024 * 1024

TM_IN, TN_IN = 256, 768
VMEM_LIMIT_IN = 60 * 1024 * 1024
TM_WPREP = 256
TQ, TK = 2048, 256
FLASH_CHAINS = 2
TM_MIX = 512
HALO = 16


def _rms(x, gain):
    ms = jnp.mean(x * x, axis=-1, keepdims=True)
    return x * lax.rsqrt(ms + EPS) * gain


def _silu(z):
    return z * jax.nn.sigmoid(z)


def _layer_spec(shape, buffers=None):
    mode = {} if buffers is None else {"pipeline_mode": pl.Buffered(buffers)}
    return pl.BlockSpec((None,) + shape, lambda *a: (a[-1][0],) + (0,) * len(shape), **mode)


def _params(semantics):
    return pltpu.CompilerParams(dimension_semantics=semantics, vmem_limit_bytes=VMEM_LIMIT)


def _in_proj_kernel(lid_ref, x_ref, g_ref, w_ref, cos_ref, sin_ref, gqn_ref, gqr_ref, gkv_ref,
                    wukv_ref, gkn_ref, gkr_ref, kpad_ref, gsgu_ref, ws_ref, bs_ref, gout_ref,
                    o_ref, q_out, k_out, v_out, hn_ref, ya_ref):
    tm = x_ref.shape[0]
    x = x_ref[...]
    hn_ref[...] = (x * g_ref[...]).astype(BF16)
    rs = lax.rsqrt(jnp.mean(x * x, axis=-1, keepdims=True) + EPS)

    def project(col):
        return rs * jnp.dot(hn_ref[...], w_ref[:, col:col + TN_IN], preferred_element_type=F32)

    bc_seen = {}
    a_seen = {}

    def mix_chunk(j):
        lo = j * TN_IN
        res = project(lo)
        for a in range(lo, lo + TN_IN, SEG):
            val = res[:, a - lo:a - lo + SEG]
            if any(start <= a < start + width for start, width in GATE_COLS):
                val = _silu(val)
            if a < COL_BB:
                a_seen[a] = val
            elif COL_BC <= a < COL_BC + B_WIDTH:
                bc_seen[a - COL_BC] = val
            elif COL_BH <= a < COL_BH + B_WIDTH:
                val = val * bc_seen[a - COL_BH]
            o_ref[:, a:a + SEG] = val.astype(o_ref.dtype)

    n_mix = MIX_COLS // TN_IN
    attn = [project(MIX_COLS + k * TN_IN) for k in range(PREP_COLS // TN_IN)]

    def piece(col, width):
        k, off = divmod(col - MIX_COLS, TN_IN)
        assert off + width <= TN_IN
        return attn[k][:, off:off + width]

    cos = cos_ref[...]
    sin = sin_ref[...]
    lane = lax.broadcasted_iota(jnp.int32, (tm, 128), 1)
    low_half = (lane & 32) == 0
    first = lane < 64

    def rope(x):
        swapped = jnp.where(low_half, pltpu.roll(x, 96, 1), pltpu.roll(x, 32, 1))
        return x * cos + swapped * sin

    cn = _rms(piece(COL_CKV, KV_RANK), gkv_ref[...]).astype(BF16)
    mix_chunk(0)
    mix_chunk(1)
    kv = jnp.dot(cn, wukv_ref[...], preferred_element_type=F32)

    def spatial_gating():
        def a_piece(col, rows, hd):
            seg, off = divmod(hd * A_HEAD, SEG)
            return a_seen[col + seg * SEG][rows, off:off + A_HEAD]

        for c in range(tm // CHUNK):
            rows = slice(c * CHUNK, (c + 1) * CHUNK)
            for hd in range(A_HEADS):
                vn = _rms(a_piece(COL_AV, rows, hd), gsgu_ref[hd:hd + 1, :]).astype(BF16)
                s = jnp.dot(ws_ref[hd], vn, preferred_element_type=F32) + bs_ref[hd]
                gated = a_piece(COL_AU, rows, hd) * s * a_piece(COL_AZ, rows, hd)
                ya_ref[rows, hd * A_HEAD:(hd + 1) * A_HEAD] = gated
        o_ref[:, COL_AU:COL_AU + A_WIDTH] = _rms(ya_ref[...], gout_ref[:, 0:A_WIDTH]).astype(o_ref.dtype)

    for j in range(2, n_mix):
        mix_chunk(j)
        if j == n_mix - 3:
            spatial_gating()

    for hd in range(C_HEADS):
        x = piece(COL_QN + hd * C_NOPE, C_NOPE)
        q_out[hd, :, 0:C_NOPE] = (_rms(x, gqn_ref[...]) * Q_SCALE).astype(BF16)

    q_pad = jnp.where(lane == C_ROPE, 1.0, 0.0)
    for j in range(C_HEADS // 2):
        x = piece(COL_QR + j * 128, 128)
        x2 = x * x
        s_lo = jnp.sum(jnp.where(first, x2, 0.0), axis=-1, keepdims=True)
        s_hi = jnp.sum(jnp.where(first, 0.0, x2), axis=-1, keepdims=True)
        ms = jnp.where(first, s_lo, s_hi) * (1.0 / C_ROPE)
        r = rope(x * lax.rsqrt(ms + EPS) * gqr_ref[...]) * Q_SCALE
        q_out[2 * j, :, C_NOPE:QK_PAD] = jnp.where(first, r, q_pad).astype(BF16)
        q_out[2 * j + 1, :, C_NOPE:QK_PAD] = jnp.where(first, pltpu.roll(r, 64, 1), q_pad).astype(BF16)

    kr = rope(_rms(piece(COL_KR, 128), gkr_ref[...]))
    kr = jnp.where(first, kr, kpad_ref[...]).astype(BF16)
    ones_col = jnp.where(lane == 0, 1.0, 0.0).astype(BF16)
    for hd in range(C_HEADS):
        base = hd * (C_NOPE + C_V)
        kn = kv[:, base:base + C_NOPE]
        k_out[hd, :, 0:C_NOPE] = _rms(kn, gkn_ref[...]).astype(BF16)
        k_out[hd, :, C_NOPE:QK_PAD] = kr
        v_out[hd, :, 0:C_V] = kv[:, base + C_NOPE:base + C_NOPE + C_V].astype(BF16)
        v_out[hd, :, C_V:V_PAD] = ones_col


def _in_proj(lid, h, gain, w, cos128, sin128, gqn, gqr, gkv, wukv, gkn, gkr, kpad, gsgu, ws, bs, gout,
             bsz, s_len):
    m = h.shape[0]
    tm = TM_IN
    nst = s_len // tm
    rows = lambda width: pl.BlockSpec((tm, width), lambda i, l: (i, 0))
    head_out = lambda width: pl.BlockSpec((None, C_HEADS, tm, width), lambda i, l: (i // nst, 0, i % nst, 0))
    head_shape = lambda width: jax.ShapeDtypeStruct((bsz, C_HEADS, s_len, width), BF16)
    return pl.pallas_call(
        _in_proj_kernel,
        grid_spec=pltpu.PrefetchScalarGridSpec(
            num_scalar_prefetch=1,
            grid=(m // tm,),
            in_specs=[
                rows(D_MODEL), _layer_spec((1, D_MODEL)), _layer_spec((D_MODEL, IN_PAD), 1),
                rows(128), rows(128),
                _layer_spec((1, C_NOPE)), _layer_spec((1, 128)), _layer_spec((1, KV_RANK)),
                _layer_spec((KV_RANK, 2 * C_WIDTH), 1), _layer_spec((1, C_NOPE)), _layer_spec((1, 128)),
                _layer_spec((1, 128)),
                _layer_spec((A_HEADS, A_HEAD)), _layer_spec((A_HEADS, CHUNK, CHUNK)),
                _layer_spec((A_HEADS, CHUNK, A_HEAD)), _layer_spec((1, D_MODEL)),
            ],
            out_specs=[rows(MIX_COLS), head_out(QK_PAD), head_out(QK_PAD), head_out(V_PAD)],
            scratch_shapes=[pltpu.VMEM((tm, D_MODEL), BF16), pltpu.VMEM((tm, A_WIDTH), F32)],
        ),
        out_shape=[jax.ShapeDtypeStruct((m, MIX_COLS), BF16),
                   head_shape(QK_PAD), head_shape(QK_PAD), head_shape(V_PAD)],
        compiler_params=pltpu.CompilerParams(dimension_semantics=("arbitrary",), vmem_limit_bytes=VMEM_LIMIT_IN),
        name="in_proj",
    )(lid, h, gain, w, cos128, sin128, gqn, gqr, gkv, wukv, gkn, gkr, kpad, gsgu, ws, bs, gout)


_NT = (((1,), (1,)), ((), ()))


def _flash_bounded_kernel(q_ref, k_ref, v_ref, o_ref):
    rows = q_ref.shape[0] // FLASH_CHAINS
    s_len = k_ref.shape[0]
    nk = s_len // TK

    def probs(c, j):
        q = q_ref[c * rows:(c + 1) * rows, :]
        s = lax.dot_general(q, k_ref[j * TK:(j + 1) * TK, :], _NT, preferred_element_type=F32)
        return jnp.exp2(s).astype(BF16)

    p = [None] * FLASH_CHAINS
    acc = [jnp.zeros((rows, V_PAD), F32)] * FLASH_CHAINS
    for t in range(-1, nk + FLASH_CHAINS - 1):
        for c in range(FLASH_CHAINS):
            j = t - c
            p_next = probs(c, j + 1) if 0 <= j + 1 < nk else None
            if 0 <= j < nk:
                acc[c] = acc[c] + jnp.dot(p[c], v_ref[j * TK:(j + 1) * TK, :], preferred_element_type=F32)
            if p_next is not None:
                p[c] = p_next
    for c in range(FLASH_CHAINS):
        o_ref[c * rows:(c + 1) * rows, :] = (acc[c][:, 0:C_V] / acc[c][:, C_V:C_V + 1]).astype(o_ref.dtype)


def _flash_online_kernel(q_ref, k_ref, v_ref, o_ref):
    tq = q_ref.shape[0]
    s_len = k_ref.shape[0]
    q = q_ref[...]

    def body(j, carry):
        m, acc = carry
        start = pl.multiple_of(j * TK, TK)
        s = lax.dot_general(q, k_ref[pl.ds(start, TK), :], _NT, preferred_element_type=F32)
        m_new = jnp.maximum(m, jnp.max(s, axis=-1, keepdims=True))
        p = jnp.exp2(s - m_new).astype(BF16)
        acc = jnp.exp2(m - m_new) * acc + jnp.dot(p, v_ref[pl.ds(start, TK), :], preferred_element_type=F32)
        return m_new, acc

    init = (jnp.full((tq, 1), -jnp.inf, F32), jnp.zeros((tq, V_PAD), F32))
    _, acc = lax.fori_loop(0, s_len // TK, body, init)
    o_ref[...] = (acc[:, 0:C_V] / acc[:, C_V:C_V + 1]).astype(o_ref.dtype)


def _flash(q, k, v, bounded):
    bsz, heads, s_len, _ = q.shape
    nq = s_len // TQ

    def call(body, name):
        return pl.pallas_call(
            body,
            grid=(bsz, heads, nq),
            in_specs=[
                pl.BlockSpec((None, None, TQ, QK_PAD), lambda b, hd, i: (b, hd, i, 0)),
                pl.BlockSpec((None, None, s_len, QK_PAD), lambda b, hd, i: (b, hd, 0, 0)),
                pl.BlockSpec((None, None, s_len, V_PAD), lambda b, hd, i: (b, hd, 0, 0)),
            ],
            out_specs=pl.BlockSpec((TQ, C_V), lambda b, hd, i: (b * nq + i, hd)),
            out_shape=jax.ShapeDtypeStruct((bsz * s_len, C_WIDTH), BF16),
            compiler_params=_params(("arbitrary", "arbitrary", "arbitrary")),
            name=name,
        )(q, k, v)

    return lax.cond(bounded,
                    lambda: call(_flash_bounded_kernel, "flash_bounded"),
                    lambda: call(_flash_online_kernel, "flash_online"))


def _mix_kernel(lid_ref, x_ref, xp_ref, xn_ref, o_ref, h_ref, p_ref,
                cw_ref, cb_ref, gout_ref, wout_ref, gple_ref, wgate_ref, wpp_ref,
                out_ref, y_ref, *, s_len):
    tm = h_ref.shape[0]
    i = pl.program_id(0)
    piece = lambda col, width: x_ref[:, col:col + width].astype(F32)

    y_ref[:, 0:A_WIDTH] = x_ref[:, COL_AU:COL_AU + A_WIDTH]

    xc = piece(COL_BH, B_WIDTH)
    row = lax.broadcasted_iota(jnp.int32, xc.shape, 0)
    has_prev = ((i * tm) % s_len != 0).astype(F32)
    has_next = (((i + 1) * tm) % s_len != 0).astype(F32)
    prev_row = xp_ref[HALO - 1:HALO, :].astype(F32) * has_prev
    next_row = xn_ref[0:1, :].astype(F32) * has_next
    x_prev = jnp.where(row == 0, prev_row, pltpu.roll(xc, 1, 0))
    x_next = jnp.where(row == tm - 1, next_row, pltpu.roll(xc, tm - 1, 0))
    conv = cb_ref[...] + x_prev * cw_ref[0:1, :] + xc * cw_ref[1:2, :] + x_next * cw_ref[2:3, :]
    yb = piece(COL_BB, B_WIDTH) * conv * piece(COL_BZ, B_WIDTH)
    y_ref[:, A_WIDTH:A_WIDTH + B_WIDTH] = _rms(yb, gout_ref[:, A_WIDTH:A_WIDTH + B_WIDTH]).astype(BF16)

    yc = o_ref[...].astype(F32) * piece(COL_CZ, C_WIDTH)
    y_ref[:, A_WIDTH + B_WIDTH:] = _rms(yc, gout_ref[:, A_WIDTH + B_WIDTH:]).astype(BF16)

    hm = h_ref[...] + jnp.dot(y_ref[...], wout_ref[...], preferred_element_type=F32)
    hn = (hm * gple_ref[...]).astype(BF16)
    rs = lax.rsqrt(jnp.mean(hm * hm, axis=-1, keepdims=True) + EPS)
    gate = jax.nn.sigmoid(rs * jnp.dot(hn, wgate_ref[...], preferred_element_type=F32))
    pp = jnp.dot(p_ref[...].astype(BF16), wpp_ref[...], preferred_element_type=F32)
    out_ref[...] = hm + gate * pp


def _mix_out(lid, proj, o, h, p, cw, cb, gout, wout, gple, wgate, wpp, s_len):
    m = h.shape[0]
    tm = TM_MIX
    nhalo = tm // HALO
    last_halo = m // HALO - 1
    halo_col = COL_BH // B_WIDTH
    return pl.pallas_call(
        functools.partial(_mix_kernel, s_len=s_len),
        grid_spec=pltpu.PrefetchScalarGridSpec(
            num_scalar_prefetch=1,
            grid=(m // tm,),
            in_specs=[
                pl.BlockSpec((tm, MIX_COLS), lambda i, l: (i, 0)),
                pl.BlockSpec((HALO, B_WIDTH), lambda i, l: (jnp.maximum(i * nhalo - 1, 0), halo_col)),
                pl.BlockSpec((HALO, B_WIDTH), lambda i, l: (jnp.minimum((i + 1) * nhalo, last_halo), halo_col)),
                pl.BlockSpec((tm, C_WIDTH), lambda i, l: (i, 0)),
                pl.BlockSpec((tm, D_MODEL), lambda i, l: (i, 0)),
                pl.BlockSpec((None, tm, PLE_DIM), lambda i, l: (l[0], i, 0)),
                _layer_spec((CONV_W, B_WIDTH)), _layer_spec((1, B_WIDTH)), _layer_spec((1, D_MODEL)),
                _layer_spec((D_MODEL, D_MODEL), 1), _layer_spec((1, D_MODEL)), _layer_spec((D_MODEL, D_MODEL), 1),
                _layer_spec((PLE_DIM, D_MODEL), 1),
            ],
            out_specs=pl.BlockSpec((tm, D_MODEL), lambda i, l: (i, 0)),
            scratch_shapes=[pltpu.VMEM((tm, D_MODEL), BF16)],
        ),
        out_shape=jax.ShapeDtypeStruct((m, D_MODEL), F32),
        compiler_params=pltpu.CompilerParams(dimension_semantics=("arbitrary",), vmem_limit_bytes=VMEM_LIMIT_IN),
        name="mix_out",
    )(lid, proj, proj, proj, o, h, p, cw, cb, gout, wout, gple, wgate, wpp)


SRC_AB = 0
SRC_Q = 3 * A_WIDTH + 4 * B_WIDTH
SRC_CKV = SRC_Q + C_HEADS * (C_NOPE + C_ROPE)
SRC_KR = SRC_CKV + KV_RANK
SRC_CZ = SRC_KR + C_ROPE
IN_WIDTH = SRC_CZ + C_WIDTH


def _w_in_kernel(wt_ref, o_ref):
    def put(dst, *pieces):
        rows = [wt_ref[src:src + n, :] for src, n in pieces]
        block = rows[0] if len(rows) == 1 else jnp.concatenate(rows, axis=0)
        o_ref[:, dst:dst + 128] = block.T.astype(BF16)

    for hd in range(C_HEADS):
        put(COL_QN + hd * C_NOPE, (SRC_Q + hd * (C_NOPE + C_ROPE), C_NOPE))
    for j in range(C_HEADS // 2):
        rope = lambda hd: (SRC_Q + hd * (C_NOPE + C_ROPE) + C_NOPE, C_ROPE)
        put(COL_QR + j * 128, rope(2 * j), rope(2 * j + 1))
    for dst, src, width in ((COL_CZ, SRC_CZ, C_WIDTH), (COL_CKV, SRC_CKV, KV_RANK), (COL_AU, SRC_AB, SRC_Q)):
        for c in range(width // 128):
            put(dst + c * 128, (src + c * 128, 128))
    put(COL_KR, (SRC_KR, C_ROPE), (SRC_KR, C_ROPE))
    o_ref[:, IN_USED:IN_PAD] = jnp.zeros((o_ref.shape[0], IN_PAD - IN_USED), BF16)


def _permute_w_in(w):
    depth = w.shape[0]
    return pl.pallas_call(
        _w_in_kernel,
        grid=(depth, D_MODEL // TM_WPREP),
        in_specs=[pl.BlockSpec((None, IN_WIDTH, TM_WPREP), lambda l, i: (l, 0, i))],
        out_specs=pl.BlockSpec((None, TM_WPREP, IN_PAD), lambda l, i: (l, i, 0)),
        out_shape=jax.ShapeDtypeStruct((depth, D_MODEL, IN_PAD), BF16),
        compiler_params=_params(("arbitrary", "arbitrary")),
        name="w_in_prep",
    )(jnp.swapaxes(w, 1, 2))


def _score_shift(gqn, gqr, gkn, gkr):
    amax = lambda g: jnp.max(jnp.abs(g), axis=-1)
    bound = Q_SCALE * (C_NOPE * amax(gqn) * amax(gkn) + C_ROPE * amax(gqr) * amax(gkr))
    shift = jnp.ceil(bound * 1.02)
    bounded = shift <= MAX_SHIFT
    return jnp.where(bounded, shift, 0.0), bounded


def kernel(x, p, positions, attn_norm, w_in, sgu_norm, w_spatial, b_spatial, conv_w, conv_b, kv_norm, w_ukv,
           q_nope_norm, q_rope_norm, k_nope_norm, k_rope_norm, out_norm, w_out, ple_norm, w_ple_gate, w_ple_proj):
    bsz, s_len, d = x.shape
    depth = w_in.shape[0]
    m = bsz * s_len

    inv = 1.0 / (ROPE_BASE ** (jnp.arange(0, C_ROPE, 2, dtype=F32) / C_ROPE))
    ang = positions.astype(F32).reshape(m, 1) * jnp.tile(inv, 4)[None, :]
    sign = jnp.tile(jnp.concatenate([-jnp.ones(C_ROPE // 2, F32), jnp.ones(C_ROPE // 2, F32)]), 2)
    cos128 = jnp.cos(ang)
    sin128 = jnp.sin(ang) * sign[None, :]

    vec = lambda g: g[:, None, :]
    w_in_p = _permute_w_in(w_in)
    w_ukv_b = w_ukv.astype(BF16)
    shift, bounded = _score_shift(q_nope_norm, q_rope_norm, k_nope_norm, k_rope_norm)
    kpad = jnp.zeros((depth, 1, 128), F32).at[:, 0, C_ROPE].set(-shift)
    gqr2 = vec(jnp.tile(q_rope_norm, (1, 2)))
    gkr2 = vec(jnp.tile(k_rope_norm, (1, 2)))
    bs_full = jnp.broadcast_to(b_spatial[..., None], (depth, A_HEADS, CHUNK, A_HEAD))
    ws_b, wout_b = w_spatial.astype(BF16), w_out.astype(BF16)
    wgate_b, wpp_b = w_ple_gate.astype(BF16), w_ple_proj.astype(BF16)
    p2 = p.reshape(depth, m, PLE_DIM)

    h = x.reshape(m, d)
    for i in range(depth):
        lid = jnp.full((1,), i, jnp.int32)
        proj, q, k, v = _in_proj(lid, h, vec(attn_norm), w_in_p, cos128, sin128, vec(q_nope_norm), gqr2,
                                 vec(kv_norm), w_ukv_b, vec(k_nope_norm), gkr2, kpad,
                                 sgu_norm, ws_b, bs_full, vec(out_norm), bsz, s_len)
        o = _flash(q, k, v, bounded[i])
        h = _mix_out(lid, proj, o, h, p2, conv_w, vec(conv_b), vec(out_norm), wout_b, vec(ple_norm),
                     wgate_b, wpp_b, s_len)
    return h.reshape(bsz, s_len, d)
```

```python
import functools
import math

import jax
import jax.numpy as jnp
from jax import lax
from jax.experimental import pallas as pl
from jax.experimental.pallas import tpu as pltpu

F32 = jnp.float32
BF16 = jnp.bfloat16

D_MODEL = 2048
PLE_DIM = 256
EPS = 1e-6
CHUNK = 128
A_HEAD = 128
A_WIDTH = 512
A_HEADS = 4
B_WIDTH = 512
CONV_W = 3
C_WIDTH = 1024
C_V = 128
C_HEADS = 8
C_NOPE = 128
C_ROPE = 64
KV_RANK = 512
ROPE_BASE = 10000.0
QK_PAD = 256
V_PAD = 256
MAX_SHIFT = 50.0

COL_AU, COL_AV, COL_AZ = 0, 512, 1024
COL_BB, COL_BC, COL_BH, COL_BZ = 1536, 2048, 2560, 3072
COL_CZ = 3584
MIX_COLS = 4608
GATE_COLS = ((COL_AZ, A_WIDTH), (COL_BZ, B_WIDTH), (COL_CZ, C_WIDTH))
SEG = 256
COL_QN = 4608
COL_QR = 5632
COL_CKV = 6144
COL_KR = 6656
IN_USED = 6784
IN_PAD = 6912
PREP_COLS = IN_PAD - MIX_COLS

Q_SCALE = (C_NOPE + C_ROPE) ** -0.5 * math.log2(math.e)

VMEM_LIMIT = 56 * 1024 * 1024

TM_IN, TN_IN = 256, 768
VMEM_LIMIT_IN = 60 * 1024 * 1024
TM_WPREP = 256
TQ, TK = 2048, 256
FLASH_CHAINS = 2
TM_MIX = 512
HALO = 16


def _rms(x, gain):
    ms = jnp.mean(x * x, axis=-1, keepdims=True)
    return x * lax.rsqrt(ms + EPS) * gain


def _sigmoid(z):
    return 0.5 * jnp.tanh(0.5 * z) + 0.5


def _silu(z):
    return z * _sigmoid(z)


def _layer_spec(shape, buffers=None):
    mode = {} if buffers is None else {"pipeline_mode": pl.Buffered(buffers)}
    return pl.BlockSpec((None,) + shape, lambda *a: (a[-1][0],) + (0,) * len(shape), **mode)


def _params(semantics):
    return pltpu.CompilerParams(dimension_semantics=semantics, vmem_limit_bytes=VMEM_LIMIT)


def _in_proj_kernel(lid_ref, x_ref, g_ref, w_ref, cos_ref, sin_ref, gqn_ref, gqr_ref, gkv_ref,
                    wukv_ref, gkn_ref, gkr_ref, kpad_ref, gsgu_ref, ws_ref, bs_ref, gout_ref,
                    o_ref, q_out, k_out, v_out, hn_ref, ya_ref):
    tm = x_ref.shape[0]
    x = x_ref[...]
    hn_ref[...] = (x * g_ref[...]).astype(BF16)
    rs = lax.rsqrt(jnp.mean(x * x, axis=-1, keepdims=True) + EPS)

    def project(col):
        return rs * jnp.dot(hn_ref[...], w_ref[:, col:col + TN_IN], preferred_element_type=F32)

    bc_seen = {}
    a_seen = {}

    def mix_chunk(j):
        lo = j * TN_IN
        res = project(lo)
        for a in range(lo, lo + TN_IN, SEG):
            val = res[:, a - lo:a - lo + SEG]
            if any(start <= a < start + width for start, width in GATE_COLS):
                val = _silu(val)
            if a < COL_BB:
                a_seen[a] = val
            elif COL_BC <= a < COL_BC + B_WIDTH:
                bc_seen[a - COL_BC] = val
            elif COL_BH <= a < COL_BH + B_WIDTH:
                val = val * bc_seen[a - COL_BH]
            o_ref[:, a:a + SEG] = val.astype(o_ref.dtype)

    n_mix = MIX_COLS // TN_IN
    attn = [project(MIX_COLS + k * TN_IN) for k in range(PREP_COLS // TN_IN)]

    def piece(col, width):
        k, off = divmod(col - MIX_COLS, TN_IN)
        assert off + width <= TN_IN
        return attn[k][:, off:off + width]

    cos = cos_ref[...]
    sin = sin_ref[...]
    lane = lax.broadcasted_iota(jnp.int32, (tm, 128), 1)
    low_half = (lane & 32) == 0
    first = lane < 64

    def rope(x):
        swapped = jnp.where(low_half, pltpu.roll(x, 96, 1), pltpu.roll(x, 32, 1))
        return x * cos + swapped * sin

    cn = _rms(piece(COL_CKV, KV_RANK), gkv_ref[...]).astype(BF16)
    mix_chunk(0)
    mix_chunk(1)
    kv = jnp.dot(cn, wukv_ref[...], preferred_element_type=F32)

    def spatial_gating():
        def a_piece(col, rows, hd):
            seg, off = divmod(hd * A_HEAD, SEG)
            return a_seen[col + seg * SEG][rows, off:off + A_HEAD]

        for c in range(tm // CHUNK):
            rows = slice(c * CHUNK, (c + 1) * CHUNK)
            for hd in range(A_HEADS):
                vn = _rms(a_piece(COL_AV, rows, hd), gsgu_ref[hd:hd + 1, :]).astype(BF16)
                s = jnp.dot(ws_ref[hd], vn, preferred_element_type=F32) + bs_ref[hd]
                gated = a_piece(COL_AU, rows, hd) * s * a_piece(COL_AZ, rows, hd)
                ya_ref[rows, hd * A_HEAD:(hd + 1) * A_HEAD] = gated
        o_ref[:, COL_AU:COL_AU + A_WIDTH] = _rms(ya_ref[...], gout_ref[:, 0:A_WIDTH]).astype(o_ref.dtype)

    for j in range(2, n_mix):
        mix_chunk(j)
        if j == n_mix - 3:
            spatial_gating()

    for hd in range(C_HEADS):
        x = piece(COL_QN + hd * C_NOPE, C_NOPE)
        q_out[hd, :, 0:C_NOPE] = (_rms(x, gqn_ref[...]) * Q_SCALE).astype(BF16)

    q_pad = jnp.where(lane == C_ROPE, 1.0, 0.0)
    for j in range(C_HEADS // 2):
        x = piece(COL_QR + j * 128, 128)
        x2 = x * x
        s_lo = jnp.sum(jnp.where(first, x2, 0.0), axis=-1, keepdims=True)
        s_hi = jnp.sum(jnp.where(first, 0.0, x2), axis=-1, keepdims=True)
        ms = jnp.where(first, s_lo, s_hi) * (1.0 / C_ROPE)
        r = rope(x * lax.rsqrt(ms + EPS) * gqr_ref[...]) * Q_SCALE
        q_out[2 * j, :, C_NOPE:QK_PAD] = jnp.where(first, r, q_pad).astype(BF16)
        q_out[2 * j + 1, :, C_NOPE:QK_PAD] = jnp.where(first, pltpu.roll(r, 64, 1), q_pad).astype(BF16)

    kr = rope(_rms(piece(COL_KR, 128), gkr_ref[...]))
    kr = jnp.where(first, kr, kpad_ref[...]).astype(BF16)
    ones_col = jnp.where(lane == 0, 1.0, 0.0).astype(BF16)
    for hd in range(C_HEADS):
        base = hd * (C_NOPE + C_V)
        kn = kv[:, base:base + C_NOPE]
        k_out[hd, :, 0:C_NOPE] = _rms(kn, gkn_ref[...]).astype(BF16)
        k_out[hd, :, C_NOPE:QK_PAD] = kr
        v_out[hd, :, 0:C_V] = kv[:, base + C_NOPE:base + C_NOPE + C_V].astype(BF16)
        v_out[hd, :, C_V:V_PAD] = ones_col


def _in_proj(lid, h, gain, w, cos128, sin128, gqn, gqr, gkv, wukv, gkn, gkr, kpad, gsgu, ws, bs, gout,
             bsz, s_len):
    m = h.shape[0]
    tm = TM_IN
    nst = s_len // tm
    rows = lambda width: pl.BlockSpec((tm, width), lambda i, l: (i, 0))
    head_out = lambda width: pl.BlockSpec((None, C_HEADS, tm, width), lambda i, l: (i // nst, 0, i % nst, 0))
    head_shape = lambda width: jax.ShapeDtypeStruct((bsz, C_HEADS, s_len, width), BF16)
    return pl.pallas_call(
        _in_proj_kernel,
        grid_spec=pltpu.PrefetchScalarGridSpec(
            num_scalar_prefetch=1,
            grid=(m // tm,),
            in_specs=[
                rows(D_MODEL), _layer_spec((1, D_MODEL)), _layer_spec((D_MODEL, IN_PAD), 1),
                rows(128), rows(128),
                _layer_spec((1, C_NOPE)), _layer_spec((1, 128)), _layer_spec((1, KV_RANK)),
                _layer_spec((KV_RANK, 2 * C_WIDTH), 1), _layer_spec((1, C_NOPE)), _layer_spec((1, 128)),
                _layer_spec((1, 128)),
                _layer_spec((A_HEADS, A_HEAD)), _layer_spec((A_HEADS, CHUNK, CHUNK)),
                _layer_spec((A_HEADS, CHUNK, A_HEAD)), _layer_spec((1, D_MODEL)),
            ],
            out_specs=[rows(MIX_COLS), head_out(QK_PAD), head_out(QK_PAD), head_out(V_PAD)],
            scratch_shapes=[pltpu.VMEM((tm, D_MODEL), BF16), pltpu.VMEM((tm, A_WIDTH), F32)],
        ),
        out_shape=[jax.ShapeDtypeStruct((m, MIX_COLS), BF16),
                   head_shape(QK_PAD), head_shape(QK_PAD), head_shape(V_PAD)],
        compiler_params=pltpu.CompilerParams(dimension_semantics=("arbitrary",), vmem_limit_bytes=VMEM_LIMIT_IN),
        name="in_proj",
    )(lid, h, gain, w, cos128, sin128, gqn, gqr, gkv, wukv, gkn, gkr, kpad, gsgu, ws, bs, gout)


_NT = (((1,), (1,)), ((), ()))


def _flash_bounded_kernel(q_ref, k_ref, v_ref, o_ref):
    rows = q_ref.shape[0] // FLASH_CHAINS
    s_len = k_ref.shape[0]
    nk = s_len // TK

    def probs(c, j):
        q = q_ref[c * rows:(c + 1) * rows, :]
        s = lax.dot_general(q, k_ref[j * TK:(j + 1) * TK, :], _NT, preferred_element_type=F32)
        return jnp.exp2(s).astype(BF16)

    p = [None] * FLASH_CHAINS
    acc = [jnp.zeros((rows, V_PAD), F32)] * FLASH_CHAINS
    for t in range(-1, nk + FLASH_CHAINS - 1):
        for c in range(FLASH_CHAINS):
            j = t - c
            p_next = probs(c, j + 1) if 0 <= j + 1 < nk else None
            if 0 <= j < nk:
                acc[c] = acc[c] + jnp.dot(p[c], v_ref[j * TK:(j + 1) * TK, :], preferred_element_type=F32)
            if p_next is not None:
                p[c] = p_next
    for c in range(FLASH_CHAINS):
        o_ref[c * rows:(c + 1) * rows, :] = (acc[c][:, 0:C_V] / acc[c][:, C_V:C_V + 1]).astype(o_ref.dtype)


def _flash_online_kernel(q_ref, k_ref, v_ref, o_ref):
    tq = q_ref.shape[0]
    s_len = k_ref.shape[0]
    q = q_ref[...]

    def body(j, carry):
        m, acc = carry
        start = pl.multiple_of(j * TK, TK)
        s = lax.dot_general(q, k_ref[pl.ds(start, TK), :], _NT, preferred_element_type=F32)
        m_new = jnp.maximum(m, jnp.max(s, axis=-1, keepdims=True))
        p = jnp.exp2(s - m_new).astype(BF16)
        acc = jnp.exp2(m - m_new) * acc + jnp.dot(p, v_ref[pl.ds(start, TK), :], preferred_element_type=F32)
        return m_new, acc

    init = (jnp.full((tq, 1), -jnp.inf, F32), jnp.zeros((tq, V_PAD), F32))
    _, acc = lax.fori_loop(0, s_len // TK, body, init)
    o_ref[...] = (acc[:, 0:C_V] / acc[:, C_V:C_V + 1]).astype(o_ref.dtype)


def _flash(q, k, v, bounded):
    bsz, heads, s_len, _ = q.shape
    nq = s_len // TQ

    def call(body, name):
        return pl.pallas_call(
            body,
            grid=(bsz, heads, nq),
            in_specs=[
                pl.BlockSpec((None, None, TQ, QK_PAD), lambda b, hd, i: (b, hd, i, 0)),
                pl.BlockSpec((None, None, s_len, QK_PAD), lambda b, hd, i: (b, hd, 0, 0)),
                pl.BlockSpec((None, None, s_len, V_PAD), lambda b, hd, i: (b, hd, 0, 0)),
            ],
            out_specs=pl.BlockSpec((TQ, C_V), lambda b, hd, i: (b * nq + i, hd)),
            out_shape=jax.ShapeDtypeStruct((bsz * s_len, C_WIDTH), BF16),
            compiler_params=_params(("arbitrary", "arbitrary", "arbitrary")),
            name=name,
        )(q, k, v)

    return lax.cond(bounded,
                    lambda: call(_flash_bounded_kernel, "flash_bounded"),
                    lambda: call(_flash_online_kernel, "flash_online"))


def _mix_kernel(lid_ref, x_ref, xp_ref, xn_ref, o_ref, h_ref, p_ref,
                cw_ref, cb_ref, gout_ref, wout_ref, gple_ref, wgate_ref, wpp_ref,
                out_ref, y_ref, *, s_len):
    tm = h_ref.shape[0]
    i = pl.program_id(0)
    piece = lambda col, width: x_ref[:, col:col + width].astype(F32)

    y_ref[:, 0:A_WIDTH] = x_ref[:, COL_AU:COL_AU + A_WIDTH]

    xc = piece(COL_BH, B_WIDTH)
    row = lax.broadcasted_iota(jnp.int32, xc.shape, 0)
    has_prev = ((i * tm) % s_len != 0).astype(F32)
    has_next = (((i + 1) * tm) % s_len != 0).astype(F32)
    prev_row = xp_ref[HALO - 1:HALO, :].astype(F32) * has_prev
    next_row = xn_ref[0:1, :].astype(F32) * has_next
    x_prev = jnp.where(row == 0, prev_row, pltpu.roll(xc, 1, 0))
    x_next = jnp.where(row == tm - 1, next_row, pltpu.roll(xc, tm - 1, 0))
    conv = cb_ref[...] + x_prev * cw_ref[0:1, :] + xc * cw_ref[1:2, :] + x_next * cw_ref[2:3, :]
    yb = piece(COL_BB, B_WIDTH) * conv * piece(COL_BZ, B_WIDTH)
    y_ref[:, A_WIDTH:A_WIDTH + B_WIDTH] = _rms(yb, gout_ref[:, A_WIDTH:A_WIDTH + B_WIDTH]).astype(BF16)

    yc = o_ref[...].astype(F32) * piece(COL_CZ, C_WIDTH)
    y_ref[:, A_WIDTH + B_WIDTH:] = _rms(yc, gout_ref[:, A_WIDTH + B_WIDTH:]).astype(BF16)

    hm = h_ref[...] + jnp.dot(y_ref[...], wout_ref[...], preferred_element_type=F32)
    hn = (hm * gple_ref[...]).astype(BF16)
    rs = lax.rsqrt(jnp.mean(hm * hm, axis=-1, keepdims=True) + EPS)
    gate = _sigmoid(rs * jnp.dot(hn, wgate_ref[...], preferred_element_type=F32))
    pp = jnp.dot(p_ref[...].astype(BF16), wpp_ref[...], preferred_element_type=F32)
    out_ref[...] = hm + gate * pp


def _mix_out(lid, proj, o, h, p, cw, cb, gout, wout, gple, wgate, wpp, s_len):
    m = h.shape[0]
    tm = TM_MIX
    nhalo = tm // HALO
    last_halo = m // HALO - 1
    halo_col = COL_BH // B_WIDTH
    return pl.pallas_call(
        functools.partial(_mix_kernel, s_len=s_len),
        grid_spec=pltpu.PrefetchScalarGridSpec(
            num_scalar_prefetch=1,
            grid=(m // tm,),
            in_specs=[
                pl.BlockSpec((tm, MIX_COLS), lambda i, l: (i, 0)),
                pl.BlockSpec((HALO, B_WIDTH), lambda i, l: (jnp.maximum(i * nhalo - 1, 0), halo_col)),
                pl.BlockSpec((HALO, B_WIDTH), lambda i, l: (jnp.minimum((i + 1) * nhalo, last_halo), halo_col)),
                pl.BlockSpec((tm, C_WIDTH), lambda i, l: (i, 0)),
                pl.BlockSpec((tm, D_MODEL), lambda i, l: (i, 0)),
                pl.BlockSpec((None, tm, PLE_DIM), lambda i, l: (l[0], i, 0)),
                _layer_spec((CONV_W, B_WIDTH)), _layer_spec((1, B_WIDTH)), _layer_spec((1, D_MODEL)),
                _layer_spec((D_MODEL, D_MODEL), 1), _layer_spec((1, D_MODEL)), _layer_spec((D_MODEL, D_MODEL), 1),
                _layer_spec((PLE_DIM, D_MODEL), 1),
            ],
            out_specs=pl.BlockSpec((tm, D_MODEL), lambda i, l: (i, 0)),
            scratch_shapes=[pltpu.VMEM((tm, D_MODEL), BF16)],
        ),
        out_shape=jax.ShapeDtypeStruct((m, D_MODEL), F32),
        compiler_params=pltpu.CompilerParams(dimension_semantics=("arbitrary",), vmem_limit_bytes=VMEM_LIMIT_IN),
        name="mix_out",
    )(lid, proj, proj, proj, o, h, p, cw, cb, gout, wout, gple, wgate, wpp)


SRC_AB = 0
SRC_Q = 3 * A_WIDTH + 4 * B_WIDTH
SRC_CKV = SRC_Q + C_HEADS * (C_NOPE + C_ROPE)
SRC_KR = SRC_CKV + KV_RANK
SRC_CZ = SRC_KR + C_ROPE
IN_WIDTH = SRC_CZ + C_WIDTH


def _w_in_kernel(wt_ref, o_ref):
    def put(dst, *pieces):
        rows = [wt_ref[src:src + n, :] for src, n in pieces]
        block = rows[0] if len(rows) == 1 else jnp.concatenate(rows, axis=0)
        o_ref[:, dst:dst + 128] = block.T.astype(BF16)

    for hd in range(C_HEADS):
        put(COL_QN + hd * C_NOPE, (SRC_Q + hd * (C_NOPE + C_ROPE), C_NOPE))
    for j in range(C_HEADS // 2):
        rope = lambda hd: (SRC_Q + hd * (C_NOPE + C_ROPE) + C_NOPE, C_ROPE)
        put(COL_QR + j * 128, rope(2 * j), rope(2 * j + 1))
    for dst, src, width in ((COL_CZ, SRC_CZ, C_WIDTH), (COL_CKV, SRC_CKV, KV_RANK), (COL_AU, SRC_AB, SRC_Q)):
        for c in range(width // 128):
            put(dst + c * 128, (src + c * 128, 128))
    put(COL_KR, (SRC_KR, C_ROPE), (SRC_KR, C_ROPE))
    o_ref[:, IN_USED:IN_PAD] = jnp.zeros((o_ref.shape[0], IN_PAD - IN_USED), BF16)


def _permute_w_in(w):
    depth = w.shape[0]
    return pl.pallas_call(
        _w_in_kernel,
        grid=(depth, D_MODEL // TM_WPREP),
        in_specs=[pl.BlockSpec((None, IN_WIDTH, TM_WPREP), lambda l, i: (l, 0, i))],
        out_specs=pl.BlockSpec((None, TM_WPREP, IN_PAD), lambda l, i: (l, i, 0)),
        out_shape=jax.ShapeDtypeStruct((depth, D_MODEL, IN_PAD), BF16),
        compiler_params=_params(("arbitrary", "arbitrary")),
        name="w_in_prep",
    )(jnp.swapaxes(w, 1, 2))


def _score_shift(gqn, gqr, gkn, gkr):
    amax = lambda g: jnp.max(jnp.abs(g), axis=-1)
    bound = Q_SCALE * (C_NOPE * amax(gqn) * amax(gkn) + C_ROPE * amax(gqr) * amax(gkr))
    shift = jnp.ceil(bound * 1.02)
    bounded = shift <= MAX_SHIFT
    return jnp.where(bounded, shift, 0.0), bounded


def kernel(x, p, positions, attn_norm, w_in, sgu_norm, w_spatial, b_spatial, conv_w, conv_b, kv_norm, w_ukv,
           q_nope_norm, q_rope_norm, k_nope_norm, k_rope_norm, out_norm, w_out, ple_norm, w_ple_gate, w_ple_proj):
    bsz, s_len, d = x.shape
    depth = w_in.shape[0]
    m = bsz * s_len

    inv = 1.0 / (ROPE_BASE ** (jnp.arange(0, C_ROPE, 2, dtype=F32) / C_ROPE))
    ang = positions.astype(F32).reshape(m, 1) * jnp.tile(inv, 4)[None, :]
    sign = jnp.tile(jnp.concatenate([-jnp.ones(C_ROPE // 2, F32), jnp.ones(C_ROPE // 2, F32)]), 2)
    cos128 = jnp.cos(ang)
    sin128 = jnp.sin(ang) * sign[None, :]

    vec = lambda g: g[:, None, :]
    w_in_p = _permute_w_in(w_in)
    w_ukv_b = w_ukv.astype(BF16)
    shift, bounded = _score_shift(q_nope_norm, q_rope_norm, k_nope_norm, k_rope_norm)
    kpad = jnp.zeros((depth, 1, 128), F32).at[:, 0, C_ROPE].set(-shift)
    gqr2 = vec(jnp.tile(q_rope_norm, (1, 2)))
    gkr2 = vec(jnp.tile(k_rope_norm, (1, 2)))
    bs_full = jnp.broadcast_to(b_spatial[..., None], (depth, A_HEADS, CHUNK, A_HEAD))
    ws_b, wout_b = w_spatial.astype(BF16), w_out.astype(BF16)
    wgate_b, wpp_b = w_ple_gate.astype(BF16), w_ple_proj.astype(BF16)
    p2 = p.reshape(depth, m, PLE_DIM)

    h = x.reshape(m, d)
    for i in range(depth):
        lid = jnp.full((1,), i, jnp.int32)
        proj, q, k, v = _in_proj(lid, h, vec(attn_norm), w_in_p, cos128, sin128, vec(q_nope_norm), gqr2,
                                 vec(kv_norm), w_ukv_b, vec(k_nope_norm), gkr2, kpad,
                                 sgu_norm, ws_b, bs_full, vec(out_norm), bsz, s_len)
        o = _flash(q, k, v, bounded[i])
        h = _mix_out(lid, proj, o, h, p2, conv_w, vec(conv_b), vec(out_norm), wout_b, vec(ple_norm),
                     wgate_b, wpp_b, s_len)
    return h.reshape(bsz, s_len, d)
```
